```python
import jax, jax.numpy as jnp
from jax import lax
import numpy as np

D_MODEL = 1024
BATCH = 16
SEQ = 2048
DEPTH = 2

GRID_W = 64
D_MIX = D_MODEL
D_ATTN = D_MIX // 2
N_HEADS = 8
HEAD_DIM = D_ATTN // N_HEADS
WIN_ROWS = 8
WIN_COLS = 16
D_POOL = D_MIX - D_ATTN
POOL_WINDOWS = (2, 4, 8, 16)
N_POOL_GROUPS = len(POOL_WINDOWS)
POOL_GROUP_DIM = D_POOL // N_POOL_GROUPS
D_IN_PROJ = 3 * D_ATTN + D_POOL
D_FF = 2816
RMS_EPS = 1e-6
NEG_INF = -1e30

kernel_name = "hybrid_natten_pool_macaron_block"


def rms_norm(x, g):
    xf = x.astype(jnp.float32)
    y = xf * lax.rsqrt(jnp.mean(xf * xf, axis=-1, keepdims=True) + RMS_EPS)
    return (y * g.astype(jnp.float32)).astype(x.dtype)


def swiglu(h, w_gate, w_up, w_down):
    return (jax.nn.silu(h @ w_gate) * (h @ w_up)) @ w_down


def neighborhood_attention(q, k, v, rpb):
    b, s, h, dh = q.shape
    rows = s // GRID_W
    kh = min(WIN_ROWS, rows)
    qg = q.reshape(b, rows, GRID_W, h, dh) * (dh ** -0.5)
    kg = k.reshape(b, rows, GRID_W, h, dh)
    vg = v.reshape(b, rows, GRID_W, h, dh)
    cols = jnp.arange(GRID_W)
    col_start = jnp.clip(cols - WIN_COLS // 2, 0, GRID_W - WIN_COLS)
    col_mask = (cols[None, :] >= col_start[:, None]) & (cols[None, :] < col_start[:, None] + WIN_COLS)
    dc_idx = jnp.clip(cols[None, :] - cols[:, None] + WIN_COLS - 1, 0, 2 * WIN_COLS - 2)
    rpb_f = rpb.astype(jnp.float32)

    def row_block(r):
        rs = jnp.clip(r - kh // 2, 0, rows - kh)
        q_r = lax.dynamic_index_in_dim(qg, r, axis=1, keepdims=False)
        k_r = lax.dynamic_slice_in_dim(kg, rs, kh, axis=1)
        v_r = lax.dynamic_slice_in_dim(vg, rs, kh, axis=1)
        dr_idx = rs + jnp.arange(kh) - r + WIN_ROWS - 1
        bias = rpb_f[:, dr_idx][:, :, dc_idx]
        bias = jnp.transpose(bias, (0, 2, 1, 3))
        scores = jnp.einsum("bqhd,bikhd->bhqik", q_r, k_r).astype(jnp.float32) + bias[None]
        scores = jnp.where(col_mask[None, None, :, None, :], scores, NEG_INF)
        probs = jax.nn.softmax(scores.reshape(b, h, GRID_W, kh * GRID_W), axis=-1)
        probs = probs.reshape(b, h, GRID_W, kh, GRID_W).astype(v.dtype)
        return jnp.einsum("bhqik,bikhd->bqhd", probs, v_r)

    out = lax.map(row_block, jnp.arange(rows))
    return jnp.transpose(out, (1, 0, 2, 3, 4)).reshape(b, s, h * dh)


def multiscale_pool(p, w_pool, scale):
    b, s, c = p.shape
    pf = p.astype(jnp.float32)
    csum = jnp.concatenate([jnp.zeros((b, 1, c), jnp.float32), jnp.cumsum(pf, axis=1)], axis=1)
    t = jnp.arange(s)
    outs = []
    for g, w in enumerate(POOL_WINDOWS):
        sl = slice(g * POOL_GROUP_DIM, (g + 1) * POOL_GROUP_DIM)
        lo = jnp.clip(t - w // 2, 0, s)
        hi = jnp.clip(t - w // 2 + w, 0, s)
        cg = csum[:, :, sl]
        mean = (cg[:, hi] - cg[:, lo]) / (hi - lo).astype(jnp.float32)[None, :, None]
        outs.append(mean - pf[:, :, sl])
    d = jnp.stack(outs, axis=2).astype(p.dtype)
    y = jnp.einsum("bsgc,gcd->bsgd", d, w_pool).reshape(b, s, c)
    return y * scale


def setup_inputs(seed: int = 0) -> dict:
    key = jax.random.key(seed)
    ks = jax.random.split(key, 20)
    f32 = jnp.float32

    def nrm(k, shape, fan_in):
        return jax.random.normal(k, shape, f32) * (fan_in ** -0.5)

    def gain(k, shape):
        return 1.0 + 0.05 * jax.random.normal(k, shape, f32)

    return {
        "x": jax.random.normal(ks[0], (BATCH, SEQ, D_MODEL), f32),
        "ffn1_norm": gain(ks[1], (DEPTH, D_MODEL)),
        "ffn1_w_gate": nrm(ks[2], (DEPTH, D_MODEL, D_FF), D_MODEL),
        "ffn1_w_up": nrm(ks[3], (DEPTH, D_MODEL, D_FF), D_MODEL),
        "ffn1_w_down": nrm(ks[4], (DEPTH, D_FF, D_MODEL), D_FF),
        "mix_norm": gain(ks[5], (DEPTH, D_MODEL)),
        "w_in": nrm(ks[6], (DEPTH, D_MODEL, D_IN_PROJ), D_MODEL),
        "rpb": 0.5 * jax.random.normal(ks[7], (DEPTH, N_HEADS, 2 * WIN_ROWS - 1, 2 * WIN_COLS - 1), f32),
        "w_pool": nrm(ks[8], (DEPTH, N_POOL_GROUPS, POOL_GROUP_DIM, POOL_GROUP_DIM), POOL_GROUP_DIM),
        "pool_scale": gain(ks[9], (DEPTH, D_POOL)),
        "w_out": nrm(ks[10], (DEPTH, D_MIX, D_MODEL), D_MIX),
        "ffn2_norm": gain(ks[11], (DEPTH, D_MODEL)),
        "ffn2_w_gate": nrm(ks[12], (DEPTH, D_MODEL, D_FF), D_MODEL),
        "ffn2_w_up": nrm(ks[13], (DEPTH, D_MODEL, D_FF), D_MODEL),
        "ffn2_w_down": nrm(ks[14], (DEPTH, D_FF, D_MODEL), D_FF),
        "final_norm": gain(ks[15], (D_MODEL,)),
    }


def reference(x, ffn1_norm, ffn1_w_gate, ffn1_w_up, ffn1_w_down, mix_norm, w_in, rpb,
              w_pool, pool_scale, w_out, ffn2_norm, ffn2_w_gate, ffn2_w_up, ffn2_w_down,
              final_norm):
    b, s, _ = x.shape
    for l in range(DEPTH):
        x = x + 0.5 * swiglu(rms_norm(x, ffn1_norm[l]), ffn1_w_gate[l], ffn1_w_up[l], ffn1_w_down[l])
        h = rms_norm(x, mix_norm[l])
        z = h @ w_in[l]
        q = z[..., 0:D_ATTN].reshape(b, s, N_HEADS, HEAD_DIM)
        k = z[..., D_ATTN:2 * D_ATTN].reshape(b, s, N_HEADS, HEAD_DIM)
        v = z[..., 2 * D_ATTN:3 * D_ATTN].reshape(b, s, N_HEADS, HEAD_DIM)
        p = z[..., 3 * D_ATTN:]
        a = neighborhood_attention(q, k, v, rpb[l])
        m = multiscale_pool(p, w_pool[l], pool_scale[l])
        x = x + jnp.concatenate([a, m], axis=-1) @ w_out[l]
        x = x + 0.5 * swiglu(rms_norm(x, ffn2_norm[l]), ffn2_w_gate[l], ffn2_w_up[l], ffn2_w_down[l])
    return rms_norm(x, final_norm)
```

```python
import functools

import jax
import jax.numpy as jnp
from jax import lax
from jax.experimental import pallas as pl
from jax.experimental.pallas import tpu as pltpu

D_MODEL = 1024
GRID_W = 64
N_HEADS = 8
HEAD_DIM = 64
D_ATTN = N_HEADS * HEAD_DIM
WIN_ROWS = 8
WIN_COLS = 16
POOL_WINDOWS = (2, 4, 8, 16)
POOL_GROUP_DIM = 128
D_POOL = POOL_GROUP_DIM * len(POOL_WINDOWS)
D_FF = 2816
RMS_EPS = 1e-6
NEG_INF = -1e30

F32 = jnp.float32
BF16 = jnp.bfloat16

LANES = 128
N_PAIRS = N_HEADS * HEAD_DIM // LANES
FF_CHUNK = 256
N_FF_CHUNKS = D_FF // FF_CHUNK
TOKEN_TILE = 512
ROW_UNROLL = 4
VMEM_LIMIT = 52 * 1024 * 1024


def _rms(x, g):
    ms = jnp.mean(x * x, axis=-1, keepdims=True)
    return x * lax.rsqrt(ms + RMS_EPS) * g


def _const_spec(shape):
    nd = len(shape)
    return pl.BlockSpec(shape, lambda *_: (0,) * nd, pipeline_mode=pl.Buffered(1))


def _ffn_kernel(x_ref, g_ref, wg_ref, wu_ref, wd_ref, fg_ref, o_ref, a_ref, *, final_norm):
    x = x_ref[...]
    h = _rms(x, g_ref[...]).astype(BF16)
    for c in range(N_FF_CHUNKS):
        gate = jnp.dot(h, wg_ref[c], preferred_element_type=F32)
        up = jnp.dot(h, wu_ref[c], preferred_element_type=F32)
        a_ref[:, c * FF_CHUNK:(c + 1) * FF_CHUNK] = (gate * jax.nn.sigmoid(gate) * up).astype(BF16)
    y = x + 0.5 * jnp.dot(a_ref[...], wd_ref[...], preferred_element_type=F32)
    if final_norm:
        y = _rms(y, fg_ref[...])
    o_ref[...] = y


def _ffn_call(x, g, wg, wu, wd, fg, final_norm):
    n = x.shape[0]
    tm = TOKEN_TILE
    return pl.pallas_call(
        functools.partial(_ffn_kernel, final_norm=final_norm),
        out_shape=jax.ShapeDtypeStruct((n, D_MODEL), F32),
        grid=(n // tm,),
        in_specs=[
            pl.BlockSpec((tm, D_MODEL), lambda i: (i, 0)),
            _const_spec((1, D_MODEL)),
            _const_spec((N_FF_CHUNKS, D_MODEL, FF_CHUNK)),
            _const_spec((N_FF_CHUNKS, D_MODEL, FF_CHUNK)),
            _const_spec((D_FF, D_MODEL)),
            _const_spec((1, D_MODEL)),
        ],
        out_specs=pl.BlockSpec((tm, D_MODEL), lambda i: (i, 0)),
        scratch_shapes=[pltpu.VMEM((tm, D_FF), BF16)],
        compiler_params=pltpu.CompilerParams(
            dimension_semantics=("parallel",), vmem_limit_bytes=VMEM_LIMIT),
        name="ffn",
    )(x, g, wg, wu, wd, fg)


def _proj_kernel(x_ref, g_ref, w_ref, q_ref, k_ref, v_ref, p_ref):
    h = _rms(x_ref[...], g_ref[...]).astype(BF16)
    z = jnp.dot(h, w_ref[...], preferred_element_type=F32)
    q_ref[...] = (z[:, 0:D_ATTN] * (HEAD_DIM ** -0.5)).astype(BF16)
    k_ref[...] = z[:, D_ATTN:2 * D_ATTN].astype(BF16)
    v_ref[...] = z[:, 2 * D_ATTN:3 * D_ATTN].astype(BF16)
    p_ref[...] = z[:, 3 * D_ATTN:]


def _proj_call(x, g, w_in):
    n = x.shape[0]
    tm = TOKEN_TILE
    tok = lambda d: pl.BlockSpec((tm, d), lambda i: (i, 0))
    return pl.pallas_call(
        _proj_kernel,
        out_shape=(
            jax.ShapeDtypeStruct((n, D_ATTN), BF16),
            jax.ShapeDtypeStruct((n, D_ATTN), BF16),
            jax.ShapeDtypeStruct((n, D_ATTN), BF16),
            jax.ShapeDtypeStruct((n, D_POOL), F32),
        ),
        grid=(n // tm,),
        in_specs=[tok(D_MODEL), _const_spec((1, D_MODEL)), _const_spec((D_MODEL, 3 * D_ATTN + D_POOL))],
        out_specs=(tok(D_ATTN), tok(D_ATTN), tok(D_ATTN), tok(D_POOL)),
        compiler_params=pltpu.CompilerParams(
            dimension_semantics=("parallel",), vmem_limit_bytes=VMEM_LIMIT),
        name="proj",
    )(x, g, w_in)


def _pool_delta(p, w):
    s = p.shape[0]
    t = lax.broadcasted_iota(jnp.int32, p.shape, 0)
    acc = p
    for j in range(-(w // 2), w // 2):
        if j == 0:
            continue
        shifted = pltpu.roll(p, (-j) % s, 0)
        valid = (t >= -j) if j < 0 else (t < s - j)
        acc = acc + jnp.where(valid, shifted, 0.0)
    cnt = jnp.minimum(t + w // 2, s) - jnp.maximum(t - w // 2, 0)
    return acc / cnt.astype(F32) - p


def _attn_pool_kernel(q_ref, k_ref, v_ref, bias_ref, p_ref, wp_ref, sc_ref, a_ref, m_ref):
    seq = q_ref.shape[0]
    rows = seq // GRID_W
    win = WIN_ROWS * GRID_W
    lane = lax.broadcasted_iota(jnp.int32, (GRID_W, LANES), 1)
    first = lane < HEAD_DIM

    def row_step(r, carry):
        rs = jnp.clip(r - WIN_ROWS // 2, 0, rows - WIN_ROWS)
        q2 = q_ref[pl.ds(pl.multiple_of(r * GRID_W, GRID_W), GRID_W), :]
        kw = k_ref[pl.ds(pl.multiple_of(rs * GRID_W, GRID_W), win), :]
        vw = v_ref[pl.ds(pl.multiple_of(rs * GRID_W, GRID_W), win), :]
        zero = jnp.zeros_like(q2)
        qq = jnp.concatenate([jnp.where(first, q2, zero), jnp.where(first, zero, q2)], axis=0)
        s = lax.dot_general(qq, kw, (((1,), (1,)), ((), ())), preferred_element_type=F32)
        s = s + bias_ref[r - rs]
        mx = jnp.max(s, axis=-1, keepdims=True)
        e = jnp.exp(s - mx)
        den = jnp.sum(e, axis=-1, keepdims=True)
        o = jnp.dot(e.astype(BF16), vw, preferred_element_type=F32) / den
        out = jnp.where(first, o[0:GRID_W], o[GRID_W:])
        a_ref[pl.ds(pl.multiple_of(r * GRID_W, GRID_W), GRID_W), :] = out.astype(BF16)
        return carry

    lax.fori_loop(0, rows, row_step, 0, unroll=ROW_UNROLL)

    group = pl.program_id(0)
    for gi, w in enumerate(POOL_WINDOWS):
        @pl.when(group == gi)
        def _(w=w):
            d = _pool_delta(p_ref[...], w).astype(BF16)
            y = jnp.dot(d, wp_ref[0], preferred_element_type=F32) * sc_ref[0]
            m_ref[...] = y.astype(BF16)


def _attn_pool_call(q, k, v, bias, p, w_pool, pool_scale, batch, seq):
    n = q.shape[0]
    tok = lambda: pl.BlockSpec((seq, LANES), lambda j, b: (b, j))
    return pl.pallas_call(
        _attn_pool_kernel,
        out_shape=(
            jax.ShapeDtypeStruct((n, D_ATTN), BF16),
            jax.ShapeDtypeStruct((n, D_POOL), BF16),
        ),
        grid=(N_PAIRS, batch),
        in_specs=[
            tok(), tok(), tok(),
            pl.BlockSpec((None, WIN_ROWS, 2 * GRID_W, WIN_ROWS * GRID_W), lambda j, b: (j, 0, 0, 0)),
            tok(),
            pl.BlockSpec((1, POOL_GROUP_DIM, POOL_GROUP_DIM), lambda j, b: (j, 0, 0)),
            pl.BlockSpec((1, 1, POOL_GROUP_DIM), lambda j, b: (j, 0, 0)),
        ],
        out_specs=(tok(), tok()),
        compiler_params=pltpu.CompilerParams(
            dimension_semantics=("parallel", "parallel"), vmem_limit_bytes=VMEM_LIMIT),
        name="attn_pool",
    )(q, k, v, bias, p, w_pool, pool_scale)


def _bias_table(rpb):
    qc = jnp.arange(GRID_W)
    kc = jnp.arange(GRID_W)
    cs = jnp.clip(qc - WIN_COLS // 2, 0, GRID_W - WIN_COLS)
    mask = (kc[None, :] >= cs[:, None]) & (kc[None, :] < cs[:, None] + WIN_COLS)
    dc = jnp.clip(kc[None, :] - qc[:, None] + WIN_COLS - 1, 0, 2 * WIN_COLS - 2)
    delta = jnp.arange(WIN_ROWS)
    dr = jnp.arange(WIN_ROWS)[None, :] - delta[:, None] + WIN_ROWS - 1
    t = rpb[:, dr][:, :, :, dc]
    t = jnp.where(mask, t, NEG_INF)
    t = jnp.transpose(t, (0, 1, 3, 2, 4)).reshape(N_HEADS, WIN_ROWS, GRID_W, WIN_ROWS * GRID_W)
    t = t.reshape(N_PAIRS, 2, WIN_ROWS, GRID_W, WIN_ROWS * GRID_W)
    return jnp.transpose(t, (0, 2, 1, 3, 4)).reshape(N_PAIRS, WIN_ROWS, 2 * GRID_W, WIN_ROWS * GRID_W)


def _outproj_kernel(x_ref, a_ref, m_ref, wa_ref, wm_ref, o_ref):
    y = jnp.dot(a_ref[...], wa_ref[...], preferred_element_type=F32)
    y = y + jnp.dot(m_ref[...], wm_ref[...], preferred_element_type=F32)
    o_ref[...] = x_ref[...] + y


def _outproj_call(x, a, m, wa, wm):
    n = x.shape[0]
    tm = TOKEN_TILE
    tok = lambda d: pl.BlockSpec((tm, d), lambda i: (i, 0))
    return pl.pallas_call(
        _outproj_kernel,
        out_shape=jax.ShapeDtypeStruct((n, D_MODEL), F32),
        grid=(n // tm,),
        in_specs=[tok(D_MODEL), tok(D_ATTN), tok(D_POOL),
                  _const_spec((D_ATTN, D_MODEL)), _const_spec((D_POOL, D_MODEL))],
        out_specs=tok(D_MODEL),
        compiler_params=pltpu.CompilerParams(
            dimension_semantics=("parallel",), vmem_limit_bytes=VMEM_LIMIT),
        name="outproj",
    )(x, a, m, wa, wm)


def _chunk_cols(w):
    d, f = w.shape
    return jnp.transpose(w.reshape(d, f // FF_CHUNK, FF_CHUNK), (1, 0, 2)).astype(BF16)


def kernel(x, ffn1_norm, ffn1_w_gate, ffn1_w_up, ffn1_w_down, mix_norm, w_in, rpb, w_pool, pool_scale, w_out, ffn2_norm, ffn2_w_gate, ffn2_w_up, ffn2_w_down, final_norm):
    batch, seq, d = x.shape
    depth = w_in.shape[0]
    x = x.reshape(batch * seq, d)
    fg = final_norm.reshape(1, d)
    for l in range(depth):
        x = _ffn_call(x, ffn1_norm[l].reshape(1, d), _chunk_cols(ffn1_w_gate[l]), _chunk_cols(ffn1_w_up[l]),
                      ffn1_w_down[l].astype(BF16), fg, False)
        q, k, v, p = _proj_call(x, mix_norm[l].reshape(1, d), w_in[l].astype(BF16))
        a, m = _attn_pool_call(q, k, v, _bias_table(rpb[l]), p, w_pool[l].astype(BF16),
                               pool_scale[l].reshape(len(POOL_WINDOWS), 1, POOL_GROUP_DIM), batch, seq)
        wo = w_out[l].astype(BF16)
        x = _outproj_call(x, a, m, wo[:D_ATTN], wo[D_ATTN:])
        x = _ffn_call(x, ffn2_norm[l].reshape(1, d), _chunk_cols(ffn2_w_gate[l]), _chunk_cols(ffn2_w_up[l]),
                      ffn2_w_down[l].astype(BF16), fg, l == depth - 1)
    return x.reshape(batch, seq, d)
```

```python
import functools

import jax
import jax.numpy as jnp
from jax import lax
from jax.experimental import pallas as pl
from jax.experimental.pallas import tpu as pltpu

D_MODEL = 1024
GRID_W = 64
N_HEADS = 8
HEAD_DIM = 64
D_ATTN = N_HEADS * HEAD_DIM
WIN_ROWS = 8
WIN_COLS = 16
POOL_WINDOWS = (2, 4, 8, 16)
POOL_GROUP_DIM = 128
D_POOL = POOL_GROUP_DIM * len(POOL_WINDOWS)
D_FF = 2816
RMS_EPS = 1e-6
NEG_INF = -1e30

F32 = jnp.float32
BF16 = jnp.bfloat16

LANES = 128
N_PAIRS = N_HEADS * HEAD_DIM // LANES
FF_CHUNK = 256
N_FF_CHUNKS = D_FF // FF_CHUNK
TOKEN_TILE = 512
PAIR_ROWS = 2
KEY_ROWS = WIN_ROWS + PAIR_ROWS - 1
STAGE_PAIRS = 2
CAST_BLOCKS = 8
LOG2E = 1.4426950408889634
VMEM_LIMIT = 52 * 1024 * 1024


def _rms(x, g):
    ms = jnp.mean(x * x, axis=-1, keepdims=True)
    return x * lax.rsqrt(ms + RMS_EPS) * g


def _layer_spec(shape, layer):
    nd = len(shape)
    return pl.BlockSpec((None,) + tuple(shape), lambda *_: (layer,) + (0,) * nd, pipeline_mode=pl.Buffered(1))


def _const_spec(shape):
    nd = len(shape)
    return pl.BlockSpec(shape, lambda *_: (0,) * nd, pipeline_mode=pl.Buffered(1))


def _cast_kernel(*refs):
    n = len(refs) // 2
    for src_ref, dst_ref in zip(refs[:n], refs[n:]):
        dst_ref[...] = src_ref[...].astype(dst_ref.dtype)


def _cast_call(weights):
    depth = weights[0].shape[0]
    specs = []
    for w in weights:
        _, r, c = w.shape
        assert r % (CAST_BLOCKS * 16) == 0
        specs.append(pl.BlockSpec((1, r // CAST_BLOCKS, c), lambda l, i: (l, i, 0)))
    return pl.pallas_call(
        _cast_kernel,
        out_shape=tuple(jax.ShapeDtypeStruct(w.shape, BF16) for w in weights),
        grid=(depth, CAST_BLOCKS),
        in_specs=specs,
        out_specs=tuple(specs),
        compiler_params=pltpu.CompilerParams(
            dimension_semantics=("parallel", "parallel"), vmem_limit_bytes=VMEM_LIMIT),
        name="cast_weights",
    )(*weights)


def _ffn_kernel(x_ref, g_ref, wg_ref, wu_ref, wd_ref, fg_ref, o_ref, a_ref, *, final_norm):
    x = x_ref[...]
    h = _rms(x, g_ref[...]).astype(BF16)
    for c in range(N_FF_CHUNKS):
        cols = slice(c * FF_CHUNK, (c + 1) * FF_CHUNK)
        gate = jnp.dot(h, wg_ref[:, cols], preferred_element_type=F32)
        up = jnp.dot(h, wu_ref[:, cols], preferred_element_type=F32)
        a_ref[:, cols] = (gate * jax.nn.sigmoid(gate) * up).astype(BF16)
    y = x + 0.5 * jnp.dot(a_ref[...], wd_ref[...], preferred_element_type=F32)
    if final_norm:
        y = _rms(y, fg_ref[...])
    o_ref[...] = y


def _ffn_call(x, g, wg, wu, wd, fg, layer, final_norm):
    n = x.shape[0]
    tm = TOKEN_TILE
    return pl.pallas_call(
        functools.partial(_ffn_kernel, final_norm=final_norm),
        out_shape=jax.ShapeDtypeStruct((n, D_MODEL), F32),
        grid=(n // tm,),
        in_specs=[
            pl.BlockSpec((tm, D_MODEL), lambda i: (i, 0)),
            _layer_spec((1, D_MODEL), layer),
            _layer_spec((D_MODEL, D_FF), layer),
            _layer_spec((D_MODEL, D_FF), layer),
            _layer_spec((D_FF, D_MODEL), layer),
            _const_spec((1, D_MODEL)),
        ],
        out_specs=pl.BlockSpec((tm, D_MODEL), lambda i: (i, 0)),
        scratch_shapes=[pltpu.VMEM((tm, D_FF), BF16)],
        compiler_params=pltpu.CompilerParams(
            dimension_semantics=("parallel",), vmem_limit_bytes=VMEM_LIMIT),
        name="ffn",
    )(x, g, wg, wu, wd, fg)


def _proj_kernel(x_ref, g_ref, w_ref, q_ref, k_ref, v_ref, p_ref):
    h = _rms(x_ref[...], g_ref[...]).astype(BF16)
    z = jnp.dot(h, w_ref[...], preferred_element_type=F32)
    q_ref[...] = (z[:, 0:D_ATTN] * (HEAD_DIM ** -0.5 * LOG2E)).astype(BF16)
    k_ref[...] = z[:, D_ATTN:2 * D_ATTN].astype(BF16)
    v_ref[...] = z[:, 2 * D_ATTN:3 * D_ATTN].astype(BF16)
    p_ref[...] = z[:, 3 * D_ATTN:]


def _proj_call(x, g, w_in, layer):
    n = x.shape[0]
    tm = TOKEN_TILE
    tok = lambda d: pl.BlockSpec((tm, d), lambda i: (i, 0))
    return pl.pallas_call(
        _proj_kernel,
        out_shape=(
            jax.ShapeDtypeStruct((n, D_ATTN), BF16),
            jax.ShapeDtypeStruct((n, D_ATTN), BF16),
            jax.ShapeDtypeStruct((n, D_ATTN), BF16),
            jax.ShapeDtypeStruct((n, D_POOL), F32),
        ),
        grid=(n // tm,),
        in_specs=[tok(D_MODEL), _layer_spec((1, D_MODEL), layer),
                  _layer_spec((D_MODEL, 3 * D_ATTN + D_POOL), layer)],
        out_specs=(tok(D_ATTN), tok(D_ATTN), tok(D_ATTN), tok(D_POOL)),
        compiler_params=pltpu.CompilerParams(
            dimension_semantics=("parallel",), vmem_limit_bytes=VMEM_LIMIT),
        name="proj",
    )(x, g, w_in)


def _window_start(r, rows):
    return min(max(r - WIN_ROWS // 2, 0), rows - WIN_ROWS)


def _pair_window_start(r0, rows):
    return min(max(r0 - WIN_ROWS // 2, 0), rows - KEY_ROWS)


def _pair_configs(rows):
    cfgs = {}
    for r0 in range(0, rows, PAIR_ROWS):
        ws = _pair_window_start(r0, rows)
        rel = tuple(_window_start(r0 + rr, rows) - ws for rr in range(PAIR_ROWS))
        assert cfgs.setdefault(r0 - ws, (r0, rel))[1] == rel
    return [(d0, r0) for d0, (r0, _) in sorted(cfgs.items())]


def _pool_delta(p, w):
    s = p.shape[0]
    t = lax.broadcasted_iota(jnp.int32, p.shape, 0)
    acc = p
    for j in range(-(w // 2), w // 2):
        if j == 0:
            continue
        shifted = pltpu.roll(p, (-j) % s, 0)
        valid = (t >= -j) if j < 0 else (t < s - j)
        acc = acc + jnp.where(valid, shifted, 0.0)
    cnt = jnp.minimum(t + w // 2, s) - jnp.maximum(t - w // 2, 0)
    return acc / cnt.astype(F32) - p


def _attn_pool_kernel(q_ref, k_ref, v_ref, bias_ref, p_ref, wp_ref, sc_ref, a_ref, m_ref,
                      st_ref, e_ref, den_ref, *, offsets):
    seq = q_ref.shape[0]
    rows = seq // GRID_W
    n_keys = KEY_ROWS * GRID_W
    n_q = PAIR_ROWS * GRID_W
    lane = lax.broadcasted_iota(jnp.int32, (n_q, LANES), 1)
    first = lane < HEAD_DIM

    def pair_geometry(rp):
        r0 = jnp.asarray(rp, jnp.int32) * PAIR_ROWS
        ws = jnp.clip(r0 - WIN_ROWS // 2, 0, rows - KEY_ROWS)
        d0 = r0 - ws
        cfg = sum((d0 >= t).astype(jnp.int32) for t in offsets[1:])
        return pl.multiple_of(r0 * GRID_W, n_q), pl.multiple_of(ws * GRID_W, GRID_W), cfg

    def scores_stage(blk, slot):
        for u in range(STAGE_PAIRS):
            tok0, key0, cfg = pair_geometry(blk * STAGE_PAIRS + u)
            qb = q_ref[pl.ds(tok0, n_q), :]
            zero = jnp.zeros_like(qb)
            qm = jnp.concatenate([jnp.where(first, qb, zero), jnp.where(first, zero, qb)], axis=0)
            kw = k_ref[pl.ds(key0, n_keys), :]
            st = lax.dot_general(kw, qm, (((1,), (1,)), ((), ())), preferred_element_type=F32)
            st_ref[slot * STAGE_PAIRS + u] = st + bias_ref[cfg]

    def softmax_stage(slot):
        for u in range(STAGE_PAIRS):
            st = st_ref[slot * STAGE_PAIRS + u]
            mx = jnp.max(st, axis=0, keepdims=True)
            e = jnp.exp2(st - mx)
            den_ref[slot * STAGE_PAIRS + u] = jnp.sum(e, axis=0, keepdims=True)
            e_ref[slot * STAGE_PAIRS + u] = e.astype(BF16)

    def output_stage(blk, slot):
        for u in range(STAGE_PAIRS):
            tok0, key0, _ = pair_geometry(blk * STAGE_PAIRS + u)
            vw = v_ref[pl.ds(key0, n_keys), :]
            ot = lax.dot_general(vw, e_ref[slot * STAGE_PAIRS + u], (((0,), (0,)), ((), ())),
                                 preferred_element_type=F32)
            o = (ot / den_ref[slot * STAGE_PAIRS + u]).T
            out = jnp.where(first, o[0:n_q], o[n_q:])
            a_ref[pl.ds(tok0, n_q), :] = out.astype(BF16)

    n_blocks = rows // PAIR_ROWS // STAGE_PAIRS
    scores_stage(0, 0)
    softmax_stage(0)
    scores_stage(1, 1)

    def steady(i, carry):
        for slot in range(2):
            t = 2 * i + slot
            scores_stage(t, slot)
            output_stage(t - 2, slot)
            softmax_stage(1 - slot)
        return carry

    assert n_blocks % 2 == 0
    lax.fori_loop(1, n_blocks // 2, steady, 0)
    output_stage(n_blocks - 2, 0)
    softmax_stage(1)
    output_stage(n_blocks - 1, 1)

    group = pl.program_id(0)
    for gi, w in enumerate(POOL_WINDOWS):
        @pl.when(group == gi)
        def _(w=w):
            d = _pool_delta(p_ref[...], w).astype(BF16)
            y = jnp.dot(d, wp_ref[...], preferred_element_type=F32) * sc_ref[...]
            m_ref[...] = y.astype(BF16)


def _attn_pool_call(q, k, v, bias, p, w_pool, pool_scale, layer, batch, seq):
    n = q.shape[0]
    offsets = tuple(d0 for d0, _ in _pair_configs(seq // GRID_W))
    tok = lambda: pl.BlockSpec((seq, LANES), lambda j, b: (b, j))
    return pl.pallas_call(
        functools.partial(_attn_pool_kernel, offsets=offsets),
        out_shape=(
            jax.ShapeDtypeStruct((n, D_ATTN), BF16),
            jax.ShapeDtypeStruct((n, D_POOL), BF16),
        ),
        grid=(N_PAIRS, batch),
        in_specs=[
            tok(), tok(), tok(),
            pl.BlockSpec((None, len(offsets), KEY_ROWS * GRID_W, 2 * PAIR_ROWS * GRID_W),
                         lambda j, b: (j, 0, 0, 0)),
            tok(),
            pl.BlockSpec((None, POOL_GROUP_DIM, POOL_GROUP_DIM), lambda j, b: (layer, j, 0)),
            pl.BlockSpec((None, None, 1, POOL_GROUP_DIM), lambda j, b: (layer, j, 0, 0)),
        ],
        out_specs=(tok(), tok()),
        scratch_shapes=[
            pltpu.VMEM((2 * STAGE_PAIRS, KEY_ROWS * GRID_W, 2 * PAIR_ROWS * GRID_W), F32),
            pltpu.VMEM((2 * STAGE_PAIRS, KEY_ROWS * GRID_W, 2 * PAIR_ROWS * GRID_W), BF16),
            pltpu.VMEM((2 * STAGE_PAIRS, 1, 2 * PAIR_ROWS * GRID_W), F32),
        ],
        compiler_params=pltpu.CompilerParams(
            dimension_semantics=("parallel", "parallel"), vmem_limit_bytes=VMEM_LIMIT),
        name="attn_pool",
    )(q, k, v, bias, p, w_pool, pool_scale)


def _bias_kernel(rpb_ref, o_ref, *, rows, configs):
    pair = pl.program_id(0)
    cfg = pl.program_id(1)
    kc = lax.broadcasted_iota(jnp.int32, (GRID_W, LANES), 0)
    lane = lax.broadcasted_iota(jnp.int32, (GRID_W, LANES), 1)
    qc = lane % GRID_W
    cs = jnp.clip(qc - WIN_COLS // 2, 0, GRID_W - WIN_COLS)
    in_cols = (kc >= cs) & (kc < cs + WIN_COLS)
    dcol = jnp.where(in_cols, kc - qc + WIN_COLS - 1, -1)
    row_stride = 2 * WIN_COLS - 1
    head_stride = (2 * WIN_ROWS - 1) * row_stride
    for ci, (d0, r0) in enumerate(configs):
        @pl.when(cfg == ci)
        def _(d0=d0, r0=r0):
            ws = r0 - d0
            for i in range(KEY_ROWS):
                key_row = ws + i
                drs = []
                for rr in range(PAIR_ROWS):
                    rs = _window_start(r0 + rr, rows)
                    drs.append(key_row - (r0 + rr) + WIN_ROWS - 1 if rs <= key_row < rs + WIN_ROWS else None)
                for hh in range(2):
                    tile = jnp.full((GRID_W, LANES), NEG_INF, F32)
                    if any(dr is not None for dr in drs):
                        base = (2 * pair + hh) * head_stride
                        for dc in range(row_stride):
                            vals = [NEG_INF if dr is None else rpb_ref[base + dr * row_stride + dc] for dr in drs]
                            tile = jnp.where(dcol == dc, jnp.where(lane < GRID_W, vals[0], vals[1]), tile)
                    o_ref[i * GRID_W:(i + 1) * GRID_W, hh * LANES:(hh + 1) * LANES] = tile * LOG2E


def _bias_call(rpb_flat, rows):
    configs = tuple(_pair_configs(rows))
    shape = (N_PAIRS, len(configs), KEY_ROWS * GRID_W, 2 * PAIR_ROWS * GRID_W)
    return pl.pallas_call(
        functools.partial(_bias_kernel, rows=rows, configs=configs),
        out_shape=jax.ShapeDtypeStruct(shape, F32),
        grid=shape[:2],
        in_specs=[pl.BlockSpec(memory_space=pltpu.SMEM)],
        out_specs=pl.BlockSpec((None, None) + shape[2:], lambda j, c: (j, c, 0, 0)),
        compiler_params=pltpu.CompilerParams(
            dimension_semantics=("parallel", "parallel"), vmem_limit_bytes=VMEM_LIMIT),
        name="bias_table",
    )(rpb_flat)


def _outproj_kernel(x_ref, a_ref, m_ref, w_ref, o_ref):
    y = jnp.dot(a_ref[...], w_ref[0:D_ATTN], preferred_element_type=F32)
    y = y + jnp.dot(m_ref[...], w_ref[D_ATTN:], preferred_element_type=F32)
    o_ref[...] = x_ref[...] + y


def _outproj_call(x, a, m, w_out, layer):
    n = x.shape[0]
    tm = TOKEN_TILE
    tok = lambda d: pl.BlockSpec((tm, d), lambda i: (i, 0))
    return pl.pallas_call(
        _outproj_kernel,
        out_shape=jax.ShapeDtypeStruct((n, D_MODEL), F32),
        grid=(n // tm,),
        in_specs=[tok(D_MODEL), tok(D_ATTN), tok(D_POOL), _layer_spec((D_ATTN + D_POOL, D_MODEL), layer)],
        out_specs=tok(D_MODEL),
        compiler_params=pltpu.CompilerParams(
            dimension_semantics=("parallel",), vmem_limit_bytes=VMEM_LIMIT),
        name="outproj",
    )(x, a, m, w_out)


def kernel(x, ffn1_norm, ffn1_w_gate, ffn1_w_up, ffn1_w_down, mix_norm, w_in, rpb, w_pool, pool_scale, w_out, ffn2_norm, ffn2_w_gate, ffn2_w_up, ffn2_w_down, final_norm):
    batch, seq, d = x.shape
    depth = w_in.shape[0]
    n_groups = len(POOL_WINDOWS)
    x = x.reshape(batch * seq, d)
    (wg1, wu1, wd1, wg2, wu2, wd2, wi, wo, wp) = _cast_call(
        (ffn1_w_gate, ffn1_w_up, ffn1_w_down, ffn2_w_gate, ffn2_w_up, ffn2_w_down, w_in, w_out,
         w_pool.reshape(depth, n_groups * POOL_GROUP_DIM, POOL_GROUP_DIM)))
    g1 = ffn1_norm.reshape(depth, 1, d)
    gm = mix_norm.reshape(depth, 1, d)
    g2 = ffn2_norm.reshape(depth, 1, d)
    fg = final_norm.reshape(1, d)
    scale = pool_scale.reshape(depth, n_groups, 1, POOL_GROUP_DIM)
    for l in range(depth):
        x = _ffn_call(x, g1, wg1, wu1, wd1, fg, l, False)
        q, k, v, p = _proj_call(x, gm, wi, l)
        bias = _bias_call(rpb[l].reshape(-1), seq // GRID_W)
        a, m = _attn_pool_call(q, k, v, bias, p, wp, scale, l, batch, seq)
        x = _outproj_call(x, a, m, wo, l)
        x = _ffn_call(x, g2, wg2, wu2, wd2, fg, l, l == depth - 1)
    return x.reshape(batch, seq, d)
```

```python
import functools

import jax
import jax.numpy as jnp
from jax import lax
from jax.experimental import pallas as pl
from jax.experimental.pallas import tpu as pltpu

D_MODEL = 1024
GRID_W = 64
N_HEADS = 8
HEAD_DIM = 64
D_ATTN = N_HEADS * HEAD_DIM
WIN_ROWS = 8
WIN_COLS = 16
POOL_WINDOWS = (2, 4, 8, 16)
POOL_GROUP_DIM = 128
D_POOL = POOL_GROUP_DIM * len(POOL_WINDOWS)
D_FF = 2816
RMS_EPS = 1e-6
NEG_INF = -1e30

F32 = jnp.float32
BF16 = jnp.bfloat16

LANES = 128
N_PAIRS = N_HEADS * HEAD_DIM // LANES
FF_CHUNK = 256
N_FF_CHUNKS = D_FF // FF_CHUNK
TOKEN_TILE = 1024
PAIR_ROWS = 2
KEY_ROWS = WIN_ROWS + PAIR_ROWS - 1
STAGE_PAIRS = 2
CAST_BLOCKS = 8
LOG2E = 1.4426950408889634
VMEM_LIMIT = 57 * 1024 * 1024


def _rms(x, g):
    ms = jnp.mean(x * x, axis=-1, keepdims=True)
    return x * lax.rsqrt(ms + RMS_EPS) * g


def _layer_spec(shape, layer):
    nd = len(shape)
    return pl.BlockSpec((None,) + tuple(shape), lambda *_: (layer,) + (0,) * nd, pipeline_mode=pl.Buffered(1))


def _const_spec(shape):
    nd = len(shape)
    return pl.BlockSpec(shape, lambda *_: (0,) * nd, pipeline_mode=pl.Buffered(1))


def _cast_kernel(*refs):
    n = len(refs) // 2
    for src_ref, dst_ref in zip(refs[:n], refs[n:]):
        dst_ref[...] = src_ref[...].astype(dst_ref.dtype)


def _cast_call(weights):
    depth = weights[0].shape[0]
    specs = []
    for w in weights:
        _, r, c = w.shape
        assert r % (CAST_BLOCKS * 16) == 0
        specs.append(pl.BlockSpec((1, r // CAST_BLOCKS, c), lambda l, i: (l, i, 0)))
    return pl.pallas_call(
        _cast_kernel,
        out_shape=tuple(jax.ShapeDtypeStruct(w.shape, BF16) for w in weights),
        grid=(depth, CAST_BLOCKS),
        in_specs=specs,
        out_specs=tuple(specs),
        compiler_params=pltpu.CompilerParams(
            dimension_semantics=("parallel", "parallel"), vmem_limit_bytes=VMEM_LIMIT),
        name="cast_weights",
    )(*weights)


def _swiglu_half_step(x, g_ref, wg_ref, wu_ref, wd_ref, a_ref):
    h = _rms(x, g_ref[...]).astype(BF16)
    for c in range(N_FF_CHUNKS):
        cols = slice(c * FF_CHUNK, (c + 1) * FF_CHUNK)
        gate = jnp.dot(h, wg_ref[:, cols], preferred_element_type=F32)
        up = jnp.dot(h, wu_ref[:, cols], preferred_element_type=F32)
        a_ref[:, cols] = (gate * jax.nn.sigmoid(gate) * up).astype(BF16)
    return x + 0.5 * jnp.dot(a_ref[...], wd_ref[...], preferred_element_type=F32)


def _ffn_proj_kernel(x_ref, g_ref, wg_ref, wu_ref, wd_ref, gm_ref, wi_ref,
                     o_ref, q_ref, k_ref, v_ref, p_ref, a_ref):
    y = _swiglu_half_step(x_ref[...], g_ref, wg_ref, wu_ref, wd_ref, a_ref)
    o_ref[...] = y
    h = _rms(y, gm_ref[...]).astype(BF16)
    z = jnp.dot(h, wi_ref[...], preferred_element_type=F32)
    q_ref[...] = (z[:, 0:D_ATTN] * (HEAD_DIM ** -0.5 * LOG2E)).astype(BF16)
    k_ref[...] = z[:, D_ATTN:2 * D_ATTN].astype(BF16)
    v_ref[...] = z[:, 2 * D_ATTN:3 * D_ATTN].astype(BF16)
    p_ref[...] = z[:, 3 * D_ATTN:]


def _ffn_proj_call(x, g, wg, wu, wd, gm, wi, layer):
    n = x.shape[0]
    tm = TOKEN_TILE
    tok = lambda d: pl.BlockSpec((tm, d), lambda i: (i, 0))
    return pl.pallas_call(
        _ffn_proj_kernel,
        out_shape=(
            jax.ShapeDtypeStruct((n, D_MODEL), F32),
            jax.ShapeDtypeStruct((n, D_ATTN), BF16),
            jax.ShapeDtypeStruct((n, D_ATTN), BF16),
            jax.ShapeDtypeStruct((n, D_ATTN), BF16),
            jax.ShapeDtypeStruct((n, D_POOL), F32),
        ),
        grid=(n // tm,),
        in_specs=[
            tok(D_MODEL),
            _layer_spec((1, D_MODEL), layer),
            _layer_spec((D_MODEL, D_FF), layer),
            _layer_spec((D_MODEL, D_FF), layer),
            _layer_spec((D_FF, D_MODEL), layer),
            _layer_spec((1, D_MODEL), layer),
            _layer_spec((D_MODEL, 3 * D_ATTN + D_POOL), layer),
        ],
        out_specs=(tok(D_MODEL), tok(D_ATTN), tok(D_ATTN), tok(D_ATTN), tok(D_POOL)),
        scratch_shapes=[pltpu.VMEM((tm, D_FF), BF16)],
        compiler_params=pltpu.CompilerParams(
            dimension_semantics=("parallel",), vmem_limit_bytes=VMEM_LIMIT),
        name="ffn_proj",
    )(x, g, wg, wu, wd, gm, wi)


def _mix_ffn_kernel(x_ref, a_in_ref, m_in_ref, wo_ref, g_ref, wg_ref, wu_ref, wd_ref, fg_ref,
                    o_ref, a_ref, *, final_norm):
    x = x_ref[...] + jnp.dot(a_in_ref[...], wo_ref[0:D_ATTN], preferred_element_type=F32)
    x = x + jnp.dot(m_in_ref[...], wo_ref[D_ATTN:], preferred_element_type=F32)
    y = _swiglu_half_step(x, g_ref, wg_ref, wu_ref, wd_ref, a_ref)
    if final_norm:
        y = _rms(y, fg_ref[...])
    o_ref[...] = y


def _mix_ffn_call(x, a, m, wo, g, wg, wu, wd, fg, layer, final_norm):
    n = x.shape[0]
    tm = TOKEN_TILE
    tok = lambda d: pl.BlockSpec((tm, d), lambda i: (i, 0))
    return pl.pallas_call(
        functools.partial(_mix_ffn_kernel, final_norm=final_norm),
        out_shape=jax.ShapeDtypeStruct((n, D_MODEL), F32),
        grid=(n // tm,),
        in_specs=[
            tok(D_MODEL), tok(D_ATTN), tok(D_POOL),
            _layer_spec((D_ATTN + D_POOL, D_MODEL), layer),
            _layer_spec((1, D_MODEL), layer),
            _layer_spec((D_MODEL, D_FF), layer),
            _layer_spec((D_MODEL, D_FF), layer),
            _layer_spec((D_FF, D_MODEL), layer),
            _const_spec((1, D_MODEL)),
        ],
        out_specs=tok(D_MODEL),
        scratch_shapes=[pltpu.VMEM((tm, D_FF), BF16)],
        compiler_params=pltpu.CompilerParams(
            dimension_semantics=("parallel",), vmem_limit_bytes=VMEM_LIMIT),
        name="mix_ffn",
    )(x, a, m, wo, g, wg, wu, wd, fg)


def _window_start(r, rows):
    return min(max(r - WIN_ROWS // 2, 0), rows - WIN_ROWS)


def _pair_window_start(r0, rows):
    return min(max(r0 - WIN_ROWS // 2, 0), rows - KEY_ROWS)


def _pair_configs(rows):
    cfgs = {}
    for r0 in range(0, rows, PAIR_ROWS):
        ws = _pair_window_start(r0, rows)
        rel = tuple(_window_start(r0 + rr, rows) - ws for rr in range(PAIR_ROWS))
        assert cfgs.setdefault(r0 - ws, (r0, rel))[1] == rel
    return [(d0, r0) for d0, (r0, _) in sorted(cfgs.items())]


def _pool_delta(p, w):
    s = p.shape[0]
    t = lax.broadcasted_iota(jnp.int32, p.shape, 0)
    acc = p
    for j in range(-(w // 2), w // 2):
        if j == 0:
            continue
        shifted = pltpu.roll(p, (-j) % s, 0)
        valid = (t >= -j) if j < 0 else (t < s - j)
        acc = acc + jnp.where(valid, shifted, 0.0)
    cnt = jnp.minimum(t + w // 2, s) - jnp.maximum(t - w // 2, 0)
    return acc / cnt.astype(F32) - p


def _attn_pool_kernel(q_ref, k_ref, v_ref, bias_ref, p_ref, wp_ref, sc_ref, a_ref, m_ref,
                      st_ref, e_ref, den_ref, *, offsets):
    seq = q_ref.shape[0]
    rows = seq // GRID_W
    n_keys = KEY_ROWS * GRID_W
    n_q = PAIR_ROWS * GRID_W
    lane = lax.broadcasted_iota(jnp.int32, (n_q, LANES), 1)
    first = lane < HEAD_DIM

    def pair_geometry(rp):
        r0 = jnp.asarray(rp, jnp.int32) * PAIR_ROWS
        ws = jnp.clip(r0 - WIN_ROWS // 2, 0, rows - KEY_ROWS)
        d0 = r0 - ws
        cfg = sum((d0 >= t).astype(jnp.int32) for t in offsets[1:])
        return pl.multiple_of(r0 * GRID_W, n_q), pl.multiple_of(ws * GRID_W, GRID_W), cfg

    def scores_stage(blk, slot):
        for u in range(STAGE_PAIRS):
            tok0, key0, cfg = pair_geometry(blk * STAGE_PAIRS + u)
            qb = q_ref[pl.ds(tok0, n_q), :]
            zero = jnp.zeros_like(qb)
            qm = jnp.concatenate([jnp.where(first, qb, zero), jnp.where(first, zero, qb)], axis=0)
            kw = k_ref[pl.ds(key0, n_keys), :]
            st = lax.dot_general(kw, qm, (((1,), (1,)), ((), ())), preferred_element_type=F32)
            st_ref[slot * STAGE_PAIRS + u] = st + bias_ref[cfg]

    def softmax_stage(slot):
        for u in range(STAGE_PAIRS):
            st = st_ref[slot * STAGE_PAIRS + u]
            mx = jnp.max(st, axis=0, keepdims=True)
            e = jnp.exp2(st - mx)
            den_ref[slot * STAGE_PAIRS + u] = jnp.sum(e, axis=0, keepdims=True)
            e_ref[slot * STAGE_PAIRS + u] = e.astype(BF16)

    def output_stage(blk, slot):
        for u in range(STAGE_PAIRS):
            tok0, key0, _ = pair_geometry(blk * STAGE_PAIRS + u)
            vw = v_ref[pl.ds(key0, n_keys), :]
            ot = lax.dot_general(vw, e_ref[slot * STAGE_PAIRS + u], (((0,), (0,)), ((), ())),
                                 preferred_element_type=F32)
            o = (ot / den_ref[slot * STAGE_PAIRS + u]).T
            out = jnp.where(first, o[0:n_q], o[n_q:])
            a_ref[pl.ds(tok0, n_q), :] = out.astype(BF16)

    n_blocks = rows // PAIR_ROWS // STAGE_PAIRS
    scores_stage(0, 0)
    softmax_stage(0)
    scores_stage(1, 1)

    def steady(i, carry):
        for slot in range(2):
            t = 2 * i + slot
            scores_stage(t, slot)
            output_stage(t - 2, slot)
            softmax_stage(1 - slot)
        return carry

    assert n_blocks % 2 == 0
    lax.fori_loop(1, n_blocks // 2, steady, 0)
    output_stage(n_blocks - 2, 0)
    softmax_stage(1)
    output_stage(n_blocks - 1, 1)

    group = pl.program_id(0)
    for gi, w in enumerate(POOL_WINDOWS):
        @pl.when(group == gi)
        def _(w=w):
            d = _pool_delta(p_ref[...], w).astype(BF16)
            y = jnp.dot(d, wp_ref[...], preferred_element_type=F32) * sc_ref[...]
            m_ref[...] = y.astype(BF16)


def _attn_pool_call(q, k, v, bias, p, w_pool, pool_scale, layer, batch, seq):
    n = q.shape[0]
    offsets = tuple(d0 for d0, _ in _pair_configs(seq // GRID_W))
    tok = lambda: pl.BlockSpec((seq, LANES), lambda j, b: (b, j))
    return pl.pallas_call(
        functools.partial(_attn_pool_kernel, offsets=offsets),
        out_shape=(
            jax.ShapeDtypeStruct((n, D_ATTN), BF16),
            jax.ShapeDtypeStruct((n, D_POOL), BF16),
        ),
        grid=(N_PAIRS, batch),
        in_specs=[
            tok(), tok(), tok(),
            pl.BlockSpec((None, len(offsets), KEY_ROWS * GRID_W, 2 * PAIR_ROWS * GRID_W),
                         lambda j, b: (j, 0, 0, 0)),
            tok(),
            pl.BlockSpec((None, POOL_GROUP_DIM, POOL_GROUP_DIM), lambda j, b: (layer, j, 0)),
            pl.BlockSpec((None, None, 1, POOL_GROUP_DIM), lambda j, b: (layer, j, 0, 0)),
        ],
        out_specs=(tok(), tok()),
        scratch_shapes=[
            pltpu.VMEM((2 * STAGE_PAIRS, KEY_ROWS * GRID_W, 2 * PAIR_ROWS * GRID_W), F32),
            pltpu.VMEM((2 * STAGE_PAIRS, KEY_ROWS * GRID_W, 2 * PAIR_ROWS * GRID_W), BF16),
            pltpu.VMEM((2 * STAGE_PAIRS, 1, 2 * PAIR_ROWS * GRID_W), F32),
        ],
        compiler_params=pltpu.CompilerParams(
            dimension_semantics=("parallel", "parallel"), vmem_limit_bytes=VMEM_LIMIT),
        name="attn_pool",
    )(q, k, v, bias, p, w_pool, pool_scale)


def _bias_kernel(rpb_ref, o_ref, *, rows, configs):
    pair = pl.program_id(0)
    cfg = pl.program_id(1)
    kc = lax.broadcasted_iota(jnp.int32, (GRID_W, LANES), 0)
    lane = lax.broadcasted_iota(jnp.int32, (GRID_W, LANES), 1)
    qc = lane % GRID_W
    cs = jnp.clip(qc - WIN_COLS // 2, 0, GRID_W - WIN_COLS)
    in_cols = (kc >= cs) & (kc < cs + WIN_COLS)
    dcol = jnp.where(in_cols, kc - qc + WIN_COLS - 1, -1)
    row_stride = 2 * WIN_COLS - 1
    head_stride = (2 * WIN_ROWS - 1) * row_stride
    for ci, (d0, r0) in enumerate(configs):
        @pl.when(cfg == ci)
        def _(d0=d0, r0=r0):
            ws = r0 - d0
            for i in range(KEY_ROWS):
                key_row = ws + i
                drs = []
                for rr in range(PAIR_ROWS):
                    rs = _window_start(r0 + rr, rows)
                    drs.append(key_row - (r0 + rr) + WIN_ROWS - 1 if rs <= key_row < rs + WIN_ROWS else None)
                for hh in range(2):
                    tile = jnp.full((GRID_W, LANES), NEG_INF, F32)
                    if any(dr is not None for dr in drs):
                        base = (2 * pair + hh) * head_stride
                        for dc in range(row_stride):
                            vals = [NEG_INF if dr is None else rpb_ref[base + dr * row_stride + dc] for dr in drs]
                            tile = jnp.where(dcol == dc, jnp.where(lane < GRID_W, vals[0], vals[1]), tile)
                    o_ref[i * GRID_W:(i + 1) * GRID_W, hh * LANES:(hh + 1) * LANES] = tile * LOG2E


def _bias_call(rpb_flat, rows):
    configs = tuple(_pair_configs(rows))
    shape = (N_PAIRS, len(configs), KEY_ROWS * GRID_W, 2 * PAIR_ROWS * GRID_W)
    return pl.pallas_call(
        functools.partial(_bias_kernel, rows=rows, configs=configs),
        out_shape=jax.ShapeDtypeStruct(shape, F32),
        grid=shape[:2],
        in_specs=[pl.BlockSpec(memory_space=pltpu.SMEM)],
        out_specs=pl.BlockSpec((None, None) + shape[2:], lambda j, c: (j, c, 0, 0)),
        compiler_params=pltpu.CompilerParams(
            dimension_semantics=("parallel", "parallel"), vmem_limit_bytes=VMEM_LIMIT),
        name="bias_table",
    )(rpb_flat)


def kernel(x, ffn1_norm, ffn1_w_gate, ffn1_w_up, ffn1_w_down, mix_norm, w_in, rpb, w_pool, pool_scale, w_out, ffn2_norm, ffn2_w_gate, ffn2_w_up, ffn2_w_down, final_norm):
    batch, seq, d = x.shape
    depth = w_in.shape[0]
    n_groups = len(POOL_WINDOWS)
    x = x.reshape(batch * seq, d)
    (wg1, wu1, wd1, wg2, wu2, wd2, wi, wo, wp) = _cast_call(
        (ffn1_w_gate, ffn1_w_up, ffn1_w_down, ffn2_w_gate, ffn2_w_up, ffn2_w_down, w_in, w_out,
         w_pool.reshape(depth, n_groups * POOL_GROUP_DIM, POOL_GROUP_DIM)))
    g1 = ffn1_norm.reshape(depth, 1, d)
    gm = mix_norm.reshape(depth, 1, d)
    g2 = ffn2_norm.reshape(depth, 1, d)
    fg = final_norm.reshape(1, d)
    scale = pool_scale.reshape(depth, n_groups, 1, POOL_GROUP_DIM)
    for l in range(depth):
        x, q, k, v, p = _ffn_proj_call(x, g1, wg1, wu1, wd1, gm, wi, l)
        bias = _bias_call(rpb[l].reshape(-1), seq // GRID_W)
        a, m = _attn_pool_call(q, k, v, bias, p, wp, scale, l, batch, seq)
        x = _mix_ffn_call(x, a, m, wo, g2, wg2, wu2, wd2, fg, l, l == depth - 1)
    return x.reshape(batch, seq, d)
```

```python
import functools

import jax
import jax.numpy as jnp
from jax import lax
from jax.experimental import pallas as pl
from jax.experimental.pallas import tpu as pltpu

D_MODEL = 1024
GRID_W = 64
N_HEADS = 8
HEAD_DIM = 64
D_ATTN = N_HEADS * HEAD_DIM
WIN_ROWS = 8
WIN_COLS = 16
POOL_WINDOWS = (2, 4, 8, 16)
POOL_GROUP_DIM = 128
D_POOL = POOL_GROUP_DIM * len(POOL_WINDOWS)
D_FF = 2816
RMS_EPS = 1e-6
NEG_INF = -1e30

F32 = jnp.float32
BF16 = jnp.bfloat16

LANES = 128
N_PAIRS = N_HEADS * HEAD_DIM // LANES
FF_CHUNK = 256
N_FF_CHUNKS = D_FF // FF_CHUNK
TOKEN_TILE = 1024
PAIR_ROWS = 2
KEY_ROWS = WIN_ROWS + PAIR_ROWS - 1
STAGE_PAIRS = 2
CAST_BLOCKS = 8
LOG2E = 1.4426950408889634
VMEM_LIMIT = 57 * 1024 * 1024


def _rms(x, g):
    ms = jnp.mean(x * x, axis=-1, keepdims=True)
    return x * lax.rsqrt(ms + RMS_EPS) * g


def _layer_spec(shape, layer):
    nd = len(shape)
    return pl.BlockSpec((None,) + tuple(shape), lambda *_: (layer,) + (0,) * nd, pipeline_mode=pl.Buffered(1))


def _const_spec(shape):
    nd = len(shape)
    return pl.BlockSpec(shape, lambda *_: (0,) * nd, pipeline_mode=pl.Buffered(1))


def _cast_kernel(*refs):
    n = len(refs) // 2
    for src_ref, dst_ref in zip(refs[:n], refs[n:]):
        dst_ref[...] = src_ref[...].astype(dst_ref.dtype)


def _cast_call(weights):
    depth = weights[0].shape[0]
    specs = []
    for w in weights:
        _, r, c = w.shape
        assert r % (CAST_BLOCKS * 16) == 0
        specs.append(pl.BlockSpec((1, r // CAST_BLOCKS, c), lambda l, i: (l, i, 0)))
    return pl.pallas_call(
        _cast_kernel,
        out_shape=tuple(jax.ShapeDtypeStruct(w.shape, BF16) for w in weights),
        grid=(depth, CAST_BLOCKS),
        in_specs=specs,
        out_specs=tuple(specs),
        compiler_params=pltpu.CompilerParams(
            dimension_semantics=("parallel", "parallel"), vmem_limit_bytes=VMEM_LIMIT),
        name="cast_weights",
    )(*weights)


def _swiglu_half_step(x, g_ref, wg_ref, wu_ref, wd_ref, a_ref):
    h = _rms(x, g_ref[...]).astype(BF16)
    for c in range(N_FF_CHUNKS):
        cols = slice(c * FF_CHUNK, (c + 1) * FF_CHUNK)
        gate = jnp.dot(h, wg_ref[:, cols], preferred_element_type=F32)
        up = jnp.dot(h, wu_ref[:, cols], preferred_element_type=F32)
        a_ref[:, cols] = (gate * jax.nn.sigmoid(gate) * up).astype(BF16)
    return x + 0.5 * jnp.dot(a_ref[...], wd_ref[...], preferred_element_type=F32)


def _ffn_proj_kernel(x_ref, g_ref, wg_ref, wu_ref, wd_ref, gm_ref, wi_ref,
                     o_ref, q_ref, k_ref, v_ref, p_ref, a_ref):
    y = _swiglu_half_step(x_ref[...], g_ref, wg_ref, wu_ref, wd_ref, a_ref)
    o_ref[...] = y
    h = _rms(y, gm_ref[...]).astype(BF16)
    z = jnp.dot(h, wi_ref[...], preferred_element_type=F32)
    for j in range(N_PAIRS):
        cols = slice(j * LANES, (j + 1) * LANES)
        q_ref[j] = (z[:, 0:D_ATTN][:, cols] * (HEAD_DIM ** -0.5 * LOG2E)).astype(BF16)
        k_ref[j] = z[:, D_ATTN:2 * D_ATTN][:, cols].astype(BF16)
        v_ref[j] = z[:, 2 * D_ATTN:3 * D_ATTN][:, cols].astype(BF16)
    for j in range(D_POOL // LANES):
        p_ref[j] = z[:, 3 * D_ATTN + j * LANES:3 * D_ATTN + (j + 1) * LANES]


def _ffn_proj_call(x, g, wg, wu, wd, gm, wi, layer):
    n = x.shape[0]
    tm = TOKEN_TILE
    tok = lambda d: pl.BlockSpec((tm, d), lambda i: (i, 0))
    blk = lambda d: pl.BlockSpec((d // LANES, tm, LANES), lambda i: (0, i, 0))
    return pl.pallas_call(
        _ffn_proj_kernel,
        out_shape=(
            jax.ShapeDtypeStruct((n, D_MODEL), F32),
            jax.ShapeDtypeStruct((N_PAIRS, n, LANES), BF16),
            jax.ShapeDtypeStruct((N_PAIRS, n, LANES), BF16),
            jax.ShapeDtypeStruct((N_PAIRS, n, LANES), BF16),
            jax.ShapeDtypeStruct((D_POOL // LANES, n, LANES), F32),
        ),
        grid=(n // tm,),
        in_specs=[
            tok(D_MODEL),
            _layer_spec((1, D_MODEL), layer),
            _layer_spec((D_MODEL, D_FF), layer),
            _layer_spec((D_MODEL, D_FF), layer),
            _layer_spec((D_FF, D_MODEL), layer),
            _layer_spec((1, D_MODEL), layer),
            _layer_spec((D_MODEL, 3 * D_ATTN + D_POOL), layer),
        ],
        out_specs=(tok(D_MODEL), blk(D_ATTN), blk(D_ATTN), blk(D_ATTN), blk(D_POOL)),
        scratch_shapes=[pltpu.VMEM((tm, D_FF), BF16)],
        compiler_params=pltpu.CompilerParams(
            dimension_semantics=("parallel",), vmem_limit_bytes=VMEM_LIMIT),
        name="ffn_proj",
    )(x, g, wg, wu, wd, gm, wi)


def _mix_ffn_kernel(x_ref, a_in_ref, m_in_ref, wo_ref, g_ref, wg_ref, wu_ref, wd_ref, fg_ref,
                    o_ref, a_ref, *, final_norm):
    mixed = jnp.concatenate([a_in_ref[j] for j in range(a_in_ref.shape[0])]
                            + [m_in_ref[j] for j in range(m_in_ref.shape[0])], axis=1)
    x = x_ref[...] + jnp.dot(mixed, wo_ref[...], preferred_element_type=F32)
    y = _swiglu_half_step(x, g_ref, wg_ref, wu_ref, wd_ref, a_ref)
    if final_norm:
        y = _rms(y, fg_ref[...])
    o_ref[...] = y


def _mix_ffn_call(x, a, m, wo, g, wg, wu, wd, fg, layer, final_norm):
    n = x.shape[0]
    tm = TOKEN_TILE
    tok = lambda d: pl.BlockSpec((tm, d), lambda i: (i, 0))
    blk = lambda d: pl.BlockSpec((d // LANES, tm, LANES), lambda i: (0, i, 0))
    return pl.pallas_call(
        functools.partial(_mix_ffn_kernel, final_norm=final_norm),
        out_shape=jax.ShapeDtypeStruct((n, D_MODEL), F32),
        grid=(n // tm,),
        in_specs=[
            tok(D_MODEL), blk(D_ATTN), blk(D_POOL),
            _layer_spec((D_ATTN + D_POOL, D_MODEL), layer),
            _layer_spec((1, D_MODEL), layer),
            _layer_spec((D_MODEL, D_FF), layer),
            _layer_spec((D_MODEL, D_FF), layer),
            _layer_spec((D_FF, D_MODEL), layer),
            _const_spec((1, D_MODEL)),
        ],
        out_specs=tok(D_MODEL),
        scratch_shapes=[pltpu.VMEM((tm, D_FF), BF16)],
        compiler_params=pltpu.CompilerParams(
            dimension_semantics=("parallel",), vmem_limit_bytes=VMEM_LIMIT),
        name="mix_ffn",
    )(x, a, m, wo, g, wg, wu, wd, fg)


def _window_start(r, rows):
    return min(max(r - WIN_ROWS // 2, 0), rows - WIN_ROWS)


def _pair_window_start(r0, rows):
    return min(max(r0 - WIN_ROWS // 2, 0), rows - KEY_ROWS)


def _pair_configs(rows):
    cfgs = {}
    for r0 in range(0, rows, PAIR_ROWS):
        ws = _pair_window_start(r0, rows)
        rel = tuple(_window_start(r0 + rr, rows) - ws for rr in range(PAIR_ROWS))
        assert cfgs.setdefault(r0 - ws, (r0, rel))[1] == rel
    return [(d0, r0) for d0, (r0, _) in sorted(cfgs.items())]


def _pool_delta(p, w):
    s = p.shape[0]
    half = w // 2
    t = lax.broadcasted_iota(jnp.int32, p.shape, 0)

    def shift(x, k):
        rolled = pltpu.roll(x, (-k) % s, 0)
        return jnp.where((t < s - k) if k > 0 else (t >= -k), rolled, 0.0)

    fwd = p
    k = 1
    while k < half:
        fwd = fwd + shift(fwd, k)
        k *= 2
    bwd = shift(p, -1)
    k = 1
    while k < half:
        bwd = bwd + shift(bwd, -k)
        k *= 2
    cnt = jnp.minimum(t + half, s) - jnp.maximum(t - half, 0)
    return (fwd + bwd) / cnt.astype(F32) - p


def _attn_pool_kernel(q_ref, k_ref, v_ref, bias_ref, p_ref, wp_ref, sc_ref, a_ref, m_ref,
                      st_ref, e_ref, den_ref, ot_ref, *, offsets):
    n_pairs, seq, _ = q_ref.shape
    rows = seq // GRID_W
    n_keys = KEY_ROWS * GRID_W
    n_q = PAIR_ROWS * GRID_W
    blocks_per_pair = rows // PAIR_ROWS // STAGE_PAIRS
    n_blocks = n_pairs * blocks_per_pair
    lane = lax.broadcasted_iota(jnp.int32, (n_q, LANES), 1)
    first = lane < HEAD_DIM

    def pair_geometry(blk, u):
        blk = jnp.asarray(blk, jnp.int32)
        pair = blk // blocks_per_pair
        r0 = ((blk % blocks_per_pair) * STAGE_PAIRS + u) * PAIR_ROWS
        ws = jnp.clip(r0 - WIN_ROWS // 2, 0, rows - KEY_ROWS)
        d0 = r0 - ws
        cfg = sum((d0 >= t).astype(jnp.int32) for t in offsets[1:])
        return pair, pl.multiple_of(r0 * GRID_W, n_q), pl.multiple_of(ws * GRID_W, GRID_W), cfg

    def scores_stage(blk, slot):
        for u in range(STAGE_PAIRS):
            pair, tok0, key0, cfg = pair_geometry(blk, u)
            qb = q_ref[pair, pl.ds(tok0, n_q), :]
            zero = jnp.zeros_like(qb)
            qm = jnp.concatenate([jnp.where(first, qb, zero), jnp.where(first, zero, qb)], axis=0)
            kw = k_ref[pair, pl.ds(key0, n_keys), :]
            st = lax.dot_general(kw, qm, (((1,), (1,)), ((), ())), preferred_element_type=F32)
            st_ref[slot * STAGE_PAIRS + u] = st + bias_ref[pair, cfg]

    def softmax_stage(slot):
        for u in range(STAGE_PAIRS):
            st = st_ref[slot * STAGE_PAIRS + u]
            mx = jnp.max(st, axis=0, keepdims=True)
            e = jnp.exp2(st - mx)
            den_ref[slot * STAGE_PAIRS + u] = jnp.sum(e, axis=0, keepdims=True)
            e_ref[slot * STAGE_PAIRS + u] = e.astype(BF16)

    def pv_stage(blk, slot):
        for u in range(STAGE_PAIRS):
            pair, _, key0, _ = pair_geometry(blk, u)
            vw = v_ref[pair, pl.ds(key0, n_keys), :]
            ot = lax.dot_general(vw, e_ref[slot * STAGE_PAIRS + u], (((0,), (0,)), ((), ())),
                                 preferred_element_type=F32)
            ot_ref[slot * STAGE_PAIRS + u] = ot / den_ref[slot * STAGE_PAIRS + u]

    def output_stage(blk, slot):
        for u in range(STAGE_PAIRS):
            pair, tok0, _, _ = pair_geometry(blk, u)
            o = ot_ref[slot * STAGE_PAIRS + u].T
            out = jnp.where(first, o[0:n_q], o[n_q:])
            a_ref[pair, pl.ds(tok0, n_q), :] = out.astype(BF16)

    def trip(t, slot):
        static = isinstance(t, int)
        if not static or 3 <= t < n_blocks + 3:
            output_stage(t - 3, 1 - slot)
        if not static or t < n_blocks:
            scores_stage(t, slot)
        if not static or 2 <= t < n_blocks + 2:
            pv_stage(t - 2, slot)
        if not static or 1 <= t < n_blocks + 1:
            softmax_stage(1 - slot)

    assert n_blocks % 2 == 0 and n_blocks >= 4
    for t in range(3):
        trip(t, t % 2)

    def steady(i, carry):
        trip(3 + 2 * i, 1)
        trip(4 + 2 * i, 0)
        return carry

    lax.fori_loop(0, (n_blocks - 4) // 2, steady, 0)
    for t in range(n_blocks - 1, n_blocks + 3):
        trip(t, t % 2)

    for gi, w in enumerate(POOL_WINDOWS):
        d = _pool_delta(p_ref[gi], w).astype(BF16)
        y = jnp.dot(d, wp_ref[gi * POOL_GROUP_DIM:(gi + 1) * POOL_GROUP_DIM], preferred_element_type=F32)
        m_ref[gi] = (y * sc_ref[gi]).astype(BF16)


def _attn_pool_call(q, k, v, bias, p, w_pool, pool_scale, layer, batch, seq):
    n = q.shape[1]
    n_groups = len(POOL_WINDOWS)
    offsets = tuple(d0 for d0, _ in _pair_configs(seq // GRID_W))
    seq_blk = lambda c: pl.BlockSpec((c, seq, LANES), lambda b: (0, b, 0))
    score_shape = (2 * STAGE_PAIRS, KEY_ROWS * GRID_W, 2 * PAIR_ROWS * GRID_W)
    return pl.pallas_call(
        functools.partial(_attn_pool_kernel, offsets=offsets),
        out_shape=(
            jax.ShapeDtypeStruct((N_PAIRS, n, LANES), BF16),
            jax.ShapeDtypeStruct((n_groups, n, LANES), BF16),
        ),
        grid=(batch,),
        in_specs=[
            seq_blk(N_PAIRS), seq_blk(N_PAIRS), seq_blk(N_PAIRS),
            _const_spec(bias.shape),
            seq_blk(n_groups),
            _layer_spec((n_groups * POOL_GROUP_DIM, POOL_GROUP_DIM), layer),
            _layer_spec((n_groups, 1, POOL_GROUP_DIM), layer),
        ],
        out_specs=(seq_blk(N_PAIRS), seq_blk(n_groups)),
        scratch_shapes=[
            pltpu.VMEM(score_shape, F32),
            pltpu.VMEM(score_shape, BF16),
            pltpu.VMEM((2 * STAGE_PAIRS, 1, score_shape[2]), F32),
            pltpu.VMEM((2 * STAGE_PAIRS, LANES, score_shape[2]), F32),
        ],
        compiler_params=pltpu.CompilerParams(
            dimension_semantics=("parallel",), vmem_limit_bytes=VMEM_LIMIT),
        name="attn_pool",
    )(q, k, v, bias, p, w_pool, pool_scale)


def _bias_kernel(rpb_ref, o_ref, *, rows, configs):
    pair = pl.program_id(0)
    cfg = pl.program_id(1)
    kc = lax.broadcasted_iota(jnp.int32, (GRID_W, LANES), 0)
    lane = lax.broadcasted_iota(jnp.int32, (GRID_W, LANES), 1)
    qc = lane % GRID_W
    cs = jnp.clip(qc - WIN_COLS // 2, 0, GRID_W - WIN_COLS)
    in_cols = (kc >= cs) & (kc < cs + WIN_COLS)
    dcol = jnp.where(in_cols, kc - qc + WIN_COLS - 1, -1)
    row_stride = 2 * WIN_COLS - 1
    head_stride = (2 * WIN_ROWS - 1) * row_stride
    for ci, (d0, r0) in enumerate(configs):
        @pl.when(cfg == ci)
        def _(d0=d0, r0=r0):
            ws = r0 - d0
            for i in range(KEY_ROWS):
                key_row = ws + i
                drs = []
                for rr in range(PAIR_ROWS):
                    rs = _window_start(r0 + rr, rows)
                    drs.append(key_row - (r0 + rr) + WIN_ROWS - 1 if rs <= key_row < rs + WIN_ROWS else None)
                for hh in range(2):
                    tile = jnp.full((GRID_W, LANES), NEG_INF, F32)
                    if any(dr is not None for dr in drs):
                        base = (2 * pair + hh) * head_stride
                        for dc in range(row_stride):
                            vals = [NEG_INF if dr is None else rpb_ref[base + dr * row_stride + dc] for dr in drs]
                            tile = jnp.where(dcol == dc, jnp.where(lane < GRID_W, vals[0], vals[1]), tile)
                    o_ref[i * GRID_W:(i + 1) * GRID_W, hh * LANES:(hh + 1) * LANES] = tile * LOG2E


def _bias_call(rpb_flat, rows):
    configs = tuple(_pair_configs(rows))
    shape = (N_PAIRS, len(configs), KEY_ROWS * GRID_W, 2 * PAIR_ROWS * GRID_W)
    return pl.pallas_call(
        functools.partial(_bias_kernel, rows=rows, configs=configs),
        out_shape=jax.ShapeDtypeStruct(shape, F32),
        grid=shape[:2],
        in_specs=[pl.BlockSpec(memory_space=pltpu.SMEM)],
        out_specs=pl.BlockSpec((None, None) + shape[2:], lambda j, c: (j, c, 0, 0)),
        compiler_params=pltpu.CompilerParams(
            dimension_semantics=("parallel", "parallel"), vmem_limit_bytes=VMEM_LIMIT),
        name="bias_table",
    )(rpb_flat)


def kernel(x, ffn1_norm, ffn1_w_gate, ffn1_w_up, ffn1_w_down, mix_norm, w_in, rpb, w_pool, pool_scale, w_out, ffn2_norm, ffn2_w_gate, ffn2_w_up, ffn2_w_down, final_norm):
    batch, seq, d = x.shape
    depth = w_in.shape[0]
    n_groups = len(POOL_WINDOWS)
    x = x.reshape(batch * seq, d)
    (wg1, wu1, wd1, wg2, wu2, wd2, wi, wo, wp) = _cast_call(
        (ffn1_w_gate, ffn1_w_up, ffn1_w_down, ffn2_w_gate, ffn2_w_up, ffn2_w_down, w_in, w_out,
         w_pool.reshape(depth, n_groups * POOL_GROUP_DIM, POOL_GROUP_DIM)))
    g1 = ffn1_norm.reshape(depth, 1, d)
    gm = mix_norm.reshape(depth, 1, d)
    g2 = ffn2_norm.reshape(depth, 1, d)
    fg = final_norm.reshape(1, d)
    scale = pool_scale.reshape(depth, n_groups, 1, POOL_GROUP_DIM)
    for l in range(depth):
        x, q, k, v, p = _ffn_proj_call(x, g1, wg1, wu1, wd1, gm, wi, l)
        bias = _bias_call(rpb[l].reshape(-1), seq // GRID_W)
        a, m = _attn_pool_call(q, k, v, bias, p, wp, scale, l, batch, seq)
        x = _mix_ffn_call(x, a, m, wo, g2, wg2, wu2, wd2, fg, l, l == depth - 1)
    return x.reshape(batch, seq, d)
```

```python
import functools

import jax
import jax.numpy as jnp
from jax import lax
from jax.experimental import pallas as pl
from jax.experimental.pallas import tpu as pltpu

D_MODEL = 1024
GRID_W = 64
N_HEADS = 8
HEAD_DIM = 64
D_ATTN = N_HEADS * HEAD_DIM
WIN_ROWS = 8
WIN_COLS = 16
POOL_WINDOWS = (2, 4, 8, 16)
POOL_GROUP_DIM = 128
D_POOL = POOL_GROUP_DIM * len(POOL_WINDOWS)
D_FF = 2816
RMS_EPS = 1e-6
NEG_INF = -1e30

F32 = jnp.float32
BF16 = jnp.bfloat16

LANES = 128
N_PAIRS = N_HEADS * HEAD_DIM // LANES
FF_CHUNK = 256
N_FF_CHUNKS = D_FF // FF_CHUNK
TOKEN_TILE = 1024
PAIR_ROWS = 2
KEY_ROWS = WIN_ROWS + PAIR_ROWS - 1
STAGE_PAIRS = 2
CAST_BLOCKS = 8
BF16_SUBLANES = 16
LOG2E = 1.4426950408889634
VMEM_LIMIT = 57 * 1024 * 1024


def _rms(x, g):
    ms = jnp.mean(x * x, axis=-1, keepdims=True)
    return x * lax.rsqrt(ms + RMS_EPS) * g


def _layer_spec(shape, layer):
    nd = len(shape)
    return pl.BlockSpec((None,) + tuple(shape), lambda *_: (layer,) + (0,) * nd, pipeline_mode=pl.Buffered(1))


def _const_spec(shape):
    nd = len(shape)
    return pl.BlockSpec(shape, lambda *_: (0,) * nd, pipeline_mode=pl.Buffered(1))


def _cast_refs(src_refs, dst_refs):
    for src_ref, dst_ref in zip(src_refs, dst_refs):
        dst_ref[...] = src_ref[...].astype(dst_ref.dtype)


def _cast_specs(weights, layer, steps):
    in_specs, out_specs, out_shapes = [], [], []
    for w in weights:
        _, r, c = w.shape
        blocks = steps
        while r % blocks or (r // blocks) % BF16_SUBLANES:
            assert blocks % 2 == 0
            blocks //= 2
        rep = steps // blocks
        in_specs.append(pl.BlockSpec((None, r // blocks, c), lambda i, rep=rep: (layer, i // rep, 0)))
        out_specs.append(pl.BlockSpec((r // blocks, c), lambda i, rep=rep: (i // rep, 0)))
        out_shapes.append(jax.ShapeDtypeStruct((r, c), BF16))
    return in_specs, out_specs, out_shapes


def _cast_kernel(*refs):
    n = len(refs) // 2
    _cast_refs(refs[:n], refs[n:])


def _cast_call(weights, layer):
    in_specs, out_specs, out_shapes = _cast_specs(weights, layer, CAST_BLOCKS)
    return pl.pallas_call(
        _cast_kernel,
        out_shape=tuple(out_shapes),
        grid=(CAST_BLOCKS,),
        in_specs=in_specs,
        out_specs=tuple(out_specs),
        compiler_params=pltpu.CompilerParams(
            dimension_semantics=("parallel",), vmem_limit_bytes=VMEM_LIMIT),
        name="cast_weights",
    )(*weights)


def _swiglu_half_step(x, g_ref, wg_ref, wu_ref, wd_ref, a_ref):
    h = _rms(x, g_ref[...]).astype(BF16)
    for c in range(N_FF_CHUNKS):
        cols = slice(c * FF_CHUNK, (c + 1) * FF_CHUNK)
        gate = jnp.dot(h, wg_ref[:, cols], preferred_element_type=F32)
        up = jnp.dot(h, wu_ref[:, cols], preferred_element_type=F32)
        a_ref[:, cols] = (gate * jax.nn.sigmoid(gate) * up).astype(BF16)
    return x + 0.5 * jnp.dot(a_ref[...], wd_ref[...], preferred_element_type=F32)


def _ffn_proj_kernel(x_ref, g_ref, wg_ref, wu_ref, wd_ref, gm_ref, wi_ref,
                     o_ref, q_ref, k_ref, v_ref, p_ref, a_ref):
    y = _swiglu_half_step(x_ref[...], g_ref, wg_ref, wu_ref, wd_ref, a_ref)
    o_ref[...] = y
    h = _rms(y, gm_ref[...]).astype(BF16)
    z = jnp.dot(h, wi_ref[...], preferred_element_type=F32)
    for j in range(N_PAIRS):
        cols = slice(j * LANES, (j + 1) * LANES)
        q_ref[j] = (z[:, 0:D_ATTN][:, cols] * (HEAD_DIM ** -0.5 * LOG2E)).astype(BF16)
        k_ref[j] = z[:, D_ATTN:2 * D_ATTN][:, cols].astype(BF16)
        v_ref[j] = z[:, 2 * D_ATTN:3 * D_ATTN][:, cols].astype(BF16)
    for j in range(D_POOL // LANES):
        p_ref[j] = z[:, 3 * D_ATTN + j * LANES:3 * D_ATTN + (j + 1) * LANES]


def _ffn_proj_call(x, g, wg, wu, wd, gm, wi, layer):
    n = x.shape[0]
    tm = TOKEN_TILE
    tok = lambda d: pl.BlockSpec((tm, d), lambda i: (i, 0))
    blk = lambda d: pl.BlockSpec((d // LANES, tm, LANES), lambda i: (0, i, 0))
    return pl.pallas_call(
        _ffn_proj_kernel,
        out_shape=(
            jax.ShapeDtypeStruct((n, D_MODEL), F32),
            jax.ShapeDtypeStruct((N_PAIRS, n, LANES), BF16),
            jax.ShapeDtypeStruct((N_PAIRS, n, LANES), BF16),
            jax.ShapeDtypeStruct((N_PAIRS, n, LANES), BF16),
            jax.ShapeDtypeStruct((D_POOL // LANES, n, LANES), F32),
        ),
        grid=(n // tm,),
        in_specs=[
            tok(D_MODEL),
            _layer_spec((1, D_MODEL), layer),
            _const_spec((D_MODEL, D_FF)),
            _const_spec((D_MODEL, D_FF)),
            _const_spec((D_FF, D_MODEL)),
            _layer_spec((1, D_MODEL), layer),
            _const_spec((D_MODEL, 3 * D_ATTN + D_POOL)),
        ],
        out_specs=(tok(D_MODEL), blk(D_ATTN), blk(D_ATTN), blk(D_ATTN), blk(D_POOL)),
        scratch_shapes=[pltpu.VMEM((tm, D_FF), BF16)],
        compiler_params=pltpu.CompilerParams(
            dimension_semantics=("parallel",), vmem_limit_bytes=VMEM_LIMIT),
        name="ffn_proj",
    )(x, g, wg, wu, wd, gm, wi)


def _mix_ffn_kernel(x_ref, a_in_ref, m_in_ref, wo_ref, g_ref, wg_ref, wu_ref, wd_ref, fg_ref, *rest,
                    final_norm, n_cast):
    cast_srcs, o_ref, cast_dsts, a_ref = rest[:n_cast], rest[n_cast], rest[n_cast + 1:-1], rest[-1]
    _cast_refs(cast_srcs, cast_dsts)
    mixed = jnp.concatenate([a_in_ref[j] for j in range(a_in_ref.shape[0])]
                            + [m_in_ref[j] for j in range(m_in_ref.shape[0])], axis=1)
    x = x_ref[...] + jnp.dot(mixed, wo_ref[...], preferred_element_type=F32)
    y = _swiglu_half_step(x, g_ref, wg_ref, wu_ref, wd_ref, a_ref)
    if final_norm:
        y = _rms(y, fg_ref[...])
    o_ref[...] = y


def _mix_ffn_call(x, a, m, wo, g, wg, wu, wd, fg, layer, final_norm, cast_srcs, cast_layer):
    n = x.shape[0]
    tm = TOKEN_TILE
    tok = lambda d: pl.BlockSpec((tm, d), lambda i: (i, 0))
    blk = lambda d: pl.BlockSpec((d // LANES, tm, LANES), lambda i: (0, i, 0))
    cast_in, cast_out, cast_shapes = _cast_specs(cast_srcs, cast_layer, n // tm)
    outs = pl.pallas_call(
        functools.partial(_mix_ffn_kernel, final_norm=final_norm, n_cast=len(cast_srcs)),
        out_shape=(jax.ShapeDtypeStruct((n, D_MODEL), F32), *cast_shapes),
        grid=(n // tm,),
        in_specs=[
            tok(D_MODEL), blk(D_ATTN), blk(D_POOL),
            _const_spec((D_ATTN + D_POOL, D_MODEL)),
            _layer_spec((1, D_MODEL), layer),
            _const_spec((D_MODEL, D_FF)),
            _const_spec((D_MODEL, D_FF)),
            _const_spec((D_FF, D_MODEL)),
            _const_spec((1, D_MODEL)),
            *cast_in,
        ],
        out_specs=(tok(D_MODEL), *cast_out),
        scratch_shapes=[pltpu.VMEM((tm, D_FF), BF16)],
        compiler_params=pltpu.CompilerParams(
            dimension_semantics=("arbitrary",), vmem_limit_bytes=VMEM_LIMIT),
        name="mix_ffn",
    )(x, a, m, wo, g, wg, wu, wd, fg, *cast_srcs)
    return outs[0], outs[1:]


def _window_start(r, rows):
    return min(max(r - WIN_ROWS // 2, 0), rows - WIN_ROWS)


def _pair_window_start(r0, rows):
    return min(max(r0 - WIN_ROWS // 2, 0), rows - KEY_ROWS)


def _pair_configs(rows):
    cfgs = {}
    for r0 in range(0, rows, PAIR_ROWS):
        ws = _pair_window_start(r0, rows)
        rel = tuple(_window_start(r0 + rr, rows) - ws for rr in range(PAIR_ROWS))
        assert cfgs.setdefault(r0 - ws, (r0, rel))[1] == rel
    return [(d0, r0) for d0, (r0, _) in sorted(cfgs.items())]


def _pool_delta(p, w):
    s = p.shape[0]
    half = w // 2
    t = lax.broadcasted_iota(jnp.int32, p.shape, 0)

    def shift(x, k):
        rolled = pltpu.roll(x, (-k) % s, 0)
        return jnp.where((t < s - k) if k > 0 else (t >= -k), rolled, 0.0)

    fwd = p
    k = 1
    while k < half:
        fwd = fwd + shift(fwd, k)
        k *= 2
    bwd = shift(p, -1)
    k = 1
    while k < half:
        bwd = bwd + shift(bwd, -k)
        k *= 2
    cnt = jnp.minimum(t + half, s) - jnp.maximum(t - half, 0)
    return (fwd + bwd) / cnt.astype(F32) - p


def _attn_pool_kernel(q_ref, k_ref, v_ref, bias_ref, p_ref, wp_ref, sc_ref, *rest, offsets, n_cast):
    cast_srcs, (a_ref, m_ref), cast_dsts = rest[:n_cast], rest[n_cast:n_cast + 2], rest[n_cast + 2:-4]
    st_ref, e_ref, den_ref, ot_ref = rest[-4:]
    _cast_refs(cast_srcs, cast_dsts)
    n_pairs, seq, _ = q_ref.shape
    rows = seq // GRID_W
    n_keys = KEY_ROWS * GRID_W
    n_q = PAIR_ROWS * GRID_W
    blocks_per_pair = rows // PAIR_ROWS // STAGE_PAIRS
    n_blocks = n_pairs * blocks_per_pair
    lane = lax.broadcasted_iota(jnp.int32, (n_q, LANES), 1)
    first = lane < HEAD_DIM

    def pair_geometry(blk, u):
        blk = jnp.asarray(blk, jnp.int32)
        pair = blk // blocks_per_pair
        r0 = ((blk % blocks_per_pair) * STAGE_PAIRS + u) * PAIR_ROWS
        ws = jnp.clip(r0 - WIN_ROWS // 2, 0, rows - KEY_ROWS)
        d0 = r0 - ws
        cfg = sum((d0 >= t).astype(jnp.int32) for t in offsets[1:])
        return pair, pl.multiple_of(r0 * GRID_W, n_q), pl.multiple_of(ws * GRID_W, GRID_W), cfg

    def scores_stage(blk, slot):
        for u in range(STAGE_PAIRS):
            pair, tok0, key0, cfg = pair_geometry(blk, u)
            qb = q_ref[pair, pl.ds(tok0, n_q), :]
            zero = jnp.zeros_like(qb)
            qm = jnp.concatenate([jnp.where(first, qb, zero), jnp.where(first, zero, qb)], axis=0)
            kw = k_ref[pair, pl.ds(key0, n_keys), :]
            st = lax.dot_general(kw, qm, (((1,), (1,)), ((), ())), preferred_element_type=F32)
            st_ref[slot * STAGE_PAIRS + u] = st + bias_ref[pair, cfg]

    def softmax_stage(slot):
        for u in range(STAGE_PAIRS):
            st = st_ref[slot * STAGE_PAIRS + u]
            mx = jnp.max(st, axis=0, keepdims=True)
            e = jnp.exp2(st - mx)
            den_ref[slot * STAGE_PAIRS + u] = jnp.sum(e, axis=0, keepdims=True)
            e_ref[slot * STAGE_PAIRS + u] = e.astype(BF16)

    def pv_stage(blk, slot):
        for u in range(STAGE_PAIRS):
            pair, _, key0, _ = pair_geometry(blk, u)
            vw = v_ref[pair, pl.ds(key0, n_keys), :]
            ot = lax.dot_general(vw, e_ref[slot * STAGE_PAIRS + u], (((0,), (0,)), ((), ())),
                                 preferred_element_type=F32)
            ot_ref[slot * STAGE_PAIRS + u] = ot / den_ref[slot * STAGE_PAIRS + u]

    def output_stage(blk, slot):
        for u in range(STAGE_PAIRS):
            pair, tok0, _, _ = pair_geometry(blk, u)
            o = ot_ref[slot * STAGE_PAIRS + u].T
            out = jnp.where(first, o[0:n_q], o[n_q:])
            a_ref[pair, pl.ds(tok0, n_q), :] = out.astype(BF16)

    def trip(t, slot):
        static = isinstance(t, int)
        if not static or 3 <= t < n_blocks + 3:
            output_stage(t - 3, 1 - slot)
        if not static or t < n_blocks:
            scores_stage(t, slot)
        if not static or 2 <= t < n_blocks + 2:
            pv_stage(t - 2, slot)
        if not static or 1 <= t < n_blocks + 1:
            softmax_stage(1 - slot)

    assert n_blocks % 2 == 0 and n_blocks >= 4
    for t in range(3):
        trip(t, t % 2)

    def steady(i, carry):
        trip(3 + 2 * i, 1)
        trip(4 + 2 * i, 0)
        return carry

    lax.fori_loop(0, (n_blocks - 4) // 2, steady, 0)
    for t in range(n_blocks - 1, n_blocks + 3):
        trip(t, t % 2)

    for gi, w in enumerate(POOL_WINDOWS):
        d = _pool_delta(p_ref[gi], w).astype(BF16)
        y = jnp.dot(d, wp_ref[gi * POOL_GROUP_DIM:(gi + 1) * POOL_GROUP_DIM], preferred_element_type=F32)
        m_ref[gi] = (y * sc_ref[gi]).astype(BF16)


def _attn_pool_call(q, k, v, bias, p, w_pool, pool_scale, layer, batch, seq, cast_srcs):
    n = q.shape[1]
    n_groups = len(POOL_WINDOWS)
    offsets = tuple(d0 for d0, _ in _pair_configs(seq // GRID_W))
    seq_blk = lambda c: pl.BlockSpec((c, seq, LANES), lambda b: (0, b, 0))
    score_shape = (2 * STAGE_PAIRS, KEY_ROWS * GRID_W, 2 * PAIR_ROWS * GRID_W)
    cast_in, cast_out, cast_shapes = _cast_specs(cast_srcs, layer, batch)
    outs = pl.pallas_call(
        functools.partial(_attn_pool_kernel, offsets=offsets, n_cast=len(cast_srcs)),
        out_shape=(
            jax.ShapeDtypeStruct((N_PAIRS, n, LANES), BF16),
            jax.ShapeDtypeStruct((n_groups, n, LANES), BF16),
            *cast_shapes,
        ),
        grid=(batch,),
        in_specs=[
            seq_blk(N_PAIRS), seq_blk(N_PAIRS), seq_blk(N_PAIRS),
            _layer_spec(bias.shape[1:], layer),
            seq_blk(n_groups),
            _const_spec((n_groups * POOL_GROUP_DIM, POOL_GROUP_DIM)),
            _layer_spec((n_groups, 1, POOL_GROUP_DIM), layer),
            *cast_in,
        ],
        out_specs=(seq_blk(N_PAIRS), seq_blk(n_groups), *cast_out),
        scratch_shapes=[
            pltpu.VMEM(score_shape, F32),
            pltpu.VMEM(score_shape, BF16),
            pltpu.VMEM((2 * STAGE_PAIRS, 1, score_shape[2]), F32),
            pltpu.VMEM((2 * STAGE_PAIRS, LANES, score_shape[2]), F32),
        ],
        compiler_params=pltpu.CompilerParams(
            dimension_semantics=("parallel",), vmem_limit_bytes=VMEM_LIMIT),
        name="attn_pool",
    )(q, k, v, bias, p, w_pool, pool_scale, *cast_srcs)
    return outs[0], outs[1], outs[2:]


def _bias_kernel(rpb_ref, o_ref, tab_ref, *, rows, configs):
    layer = pl.program_id(0)
    pair = pl.program_id(1)
    n_dr = 2 * WIN_ROWS - 1
    n_dc = 2 * WIN_COLS - 1
    sublanes = 8
    sub = lax.broadcasted_iota(jnp.int32, (sublanes, LANES), 0)
    lane8 = lax.broadcasted_iota(jnp.int32, (sublanes, LANES), 1)
    skew = (n_dc - 1) - lane8 + sub
    for hh in range(2):
        base = ((layer * N_HEADS + 2 * pair + hh) * n_dr) * n_dc
        for dr in range(n_dr):
            vec = jnp.zeros((sublanes, LANES), F32)
            for dc in range(n_dc):
                vec = jnp.where(skew == dc, rpb_ref[base + dr * n_dc + dc], vec)
            for g in range(GRID_W // sublanes):
                for rr in range(PAIR_ROWS):
                    shift = (sublanes * g - (WIN_COLS - 1) + GRID_W * rr) % LANES
                    rows8 = pl.ds(dr * GRID_W + sublanes * g, sublanes)
                    tab_ref[hh, rr, rows8, :] = pltpu.roll(vec, shift, 1)

    kc = lax.broadcasted_iota(jnp.int32, (GRID_W, LANES), 0)
    lane = lax.broadcasted_iota(jnp.int32, (GRID_W, LANES), 1)
    qc = lane % GRID_W
    cs = jnp.clip(qc - WIN_COLS // 2, 0, GRID_W - WIN_COLS)
    in_cols = (kc >= cs) & (kc < cs + WIN_COLS)
    for ci, (d0, r0) in enumerate(configs):
        ws = r0 - d0
        for i in range(KEY_ROWS):
            key_row = ws + i
            for hh in range(2):
                halves = []
                for rr in range(PAIR_ROWS):
                    rs = _window_start(r0 + rr, rows)
                    if rs <= key_row < rs + WIN_ROWS:
                        dr = key_row - (r0 + rr) + WIN_ROWS - 1
                        halves.append(tab_ref[hh, rr, dr * GRID_W:(dr + 1) * GRID_W, :])
                    else:
                        halves.append(jnp.full((GRID_W, LANES), NEG_INF, F32))
                tile = jnp.where(in_cols, jnp.where(lane < GRID_W, halves[0], halves[1]), NEG_INF)
                o_ref[ci, i * GRID_W:(i + 1) * GRID_W, hh * LANES:(hh + 1) * LANES] = tile * LOG2E


def _bias_call(rpb_flat, depth, rows):
    configs = tuple(_pair_configs(rows))
    slab = (len(configs), KEY_ROWS * GRID_W, 2 * PAIR_ROWS * GRID_W)
    return pl.pallas_call(
        functools.partial(_bias_kernel, rows=rows, configs=configs),
        out_shape=jax.ShapeDtypeStruct((depth, N_PAIRS) + slab, F32),
        grid=(depth, N_PAIRS),
        in_specs=[pl.BlockSpec(memory_space=pltpu.SMEM)],
        out_specs=pl.BlockSpec((None, None) + slab, lambda l, j: (l, j, 0, 0, 0)),
        scratch_shapes=[pltpu.VMEM((2, PAIR_ROWS, (2 * WIN_ROWS - 1) * GRID_W, LANES), F32)],
        compiler_params=pltpu.CompilerParams(
            dimension_semantics=("parallel", "parallel"), vmem_limit_bytes=VMEM_LIMIT),
        name="bias_table",
    )(rpb_flat)


def kernel(x, ffn1_norm, ffn1_w_gate, ffn1_w_up, ffn1_w_down, mix_norm, w_in, rpb, w_pool, pool_scale, w_out, ffn2_norm, ffn2_w_gate, ffn2_w_up, ffn2_w_down, final_norm):
    batch, seq, d = x.shape
    depth = w_in.shape[0]
    n_groups = len(POOL_WINDOWS)
    x = x.reshape(batch * seq, d)
    first = (ffn1_w_gate, ffn1_w_up, ffn1_w_down, w_in,
             w_pool.reshape(depth, n_groups * POOL_GROUP_DIM, POOL_GROUP_DIM))
    second = (w_out, ffn2_w_gate, ffn2_w_up, ffn2_w_down)
    g1 = ffn1_norm.reshape(depth, 1, d)
    gm = mix_norm.reshape(depth, 1, d)
    g2 = ffn2_norm.reshape(depth, 1, d)
    fg = final_norm.reshape(1, d)
    scale = pool_scale.reshape(depth, n_groups, 1, POOL_GROUP_DIM)
    bias = _bias_call(rpb.reshape(-1), depth, seq // GRID_W)
    wg1, wu1, wd1, wi, wp = _cast_call(first, 0)
    for l in range(depth):
        x, q, k, v, p = _ffn_proj_call(x, g1, wg1, wu1, wd1, gm, wi, l)
        a, m, (wo, wg2, wu2, wd2) = _attn_pool_call(q, k, v, bias, p, wp, scale, l, batch, seq, second)
        ahead = first if l + 1 < depth else ()
        x, cast = _mix_ffn_call(x, a, m, wo, g2, wg2, wu2, wd2, fg, l, l == depth - 1, ahead, l + 1)
        if ahead:
            wg1, wu1, wd1, wi, wp = cast
    return x.reshape(batch, seq, d)
```

```python
import functools

import jax
import jax.numpy as jnp
from jax import lax
from jax.experimental import pallas as pl
from jax.experimental.pallas import tpu as pltpu

D_MODEL = 1024
GRID_W = 64
N_HEADS = 8
HEAD_DIM = 64
D_ATTN = N_HEADS * HEAD_DIM
WIN_ROWS = 8
WIN_COLS = 16
POOL_WINDOWS = (2, 4, 8, 16)
POOL_GROUP_DIM = 128
D_POOL = POOL_GROUP_DIM * len(POOL_WINDOWS)
D_FF = 2816
RMS_EPS = 1e-6
NEG_INF = -1e30

F32 = jnp.float32
BF16 = jnp.bfloat16

LANES = 128
N_PAIRS = N_HEADS * HEAD_DIM // LANES
FF_CHUNK = 256
N_FF_CHUNKS = D_FF // FF_CHUNK
TOKEN_TILE = 1024
PAIR_ROWS = 2
KEY_ROWS = WIN_ROWS + PAIR_ROWS - 1
STAGE_PAIRS = 2
CAST_BLOCKS = 8
BF16_SUBLANES = 16
POOL_CHUNK = 128
POOL_HALO = max(POOL_WINDOWS) // 2
LOG2E = 1.4426950408889634
VMEM_LIMIT = 57 * 1024 * 1024


def _rms(x, g):
    ms = jnp.mean(x * x, axis=-1, keepdims=True)
    return x * lax.rsqrt(ms + RMS_EPS) * g


def _layer_spec(shape, layer):
    nd = len(shape)
    return pl.BlockSpec((None,) + tuple(shape), lambda *_: (layer,) + (0,) * nd, pipeline_mode=pl.Buffered(1))


def _const_spec(shape):
    nd = len(shape)
    return pl.BlockSpec(shape, lambda *_: (0,) * nd, pipeline_mode=pl.Buffered(1))


def _cast_refs(src_refs, dst_refs):
    for src_ref, dst_ref in zip(src_refs, dst_refs):
        dst_ref[...] = src_ref[...].astype(dst_ref.dtype)


def _cast_specs(weights, layer, steps):
    in_specs, out_specs, out_shapes = [], [], []
    for w in weights:
        _, r, c = w.shape
        blocks = steps
        while r % blocks or (r // blocks) % BF16_SUBLANES:
            assert blocks % 2 == 0
            blocks //= 2
        rep = steps // blocks
        in_specs.append(pl.BlockSpec((None, r // blocks, c), lambda i, rep=rep: (layer, i // rep, 0)))
        out_specs.append(pl.BlockSpec((r // blocks, c), lambda i, rep=rep: (i // rep, 0)))
        out_shapes.append(jax.ShapeDtypeStruct((r, c), BF16))
    return in_specs, out_specs, out_shapes


def _cast_kernel(*refs):
    n = len(refs) // 2
    _cast_refs(refs[:n], refs[n:])


def _cast_call(weights, layer):
    in_specs, out_specs, out_shapes = _cast_specs(weights, layer, CAST_BLOCKS)
    return pl.pallas_call(
        _cast_kernel,
        out_shape=tuple(out_shapes),
        grid=(CAST_BLOCKS,),
        in_specs=in_specs,
        out_specs=tuple(out_specs),
        compiler_params=pltpu.CompilerParams(
            dimension_semantics=("parallel",), vmem_limit_bytes=VMEM_LIMIT),
        name="cast_weights",
    )(*weights)


def _swiglu_half_step(x, g_ref, wg_ref, wu_ref, wd_ref, a_ref):
    h = _rms(x, g_ref[...]).astype(BF16)
    for c in range(N_FF_CHUNKS):
        cols = slice(c * FF_CHUNK, (c + 1) * FF_CHUNK)
        gate = jnp.dot(h, wg_ref[:, cols], preferred_element_type=F32)
        up = jnp.dot(h, wu_ref[:, cols], preferred_element_type=F32)
        a_ref[:, cols] = (gate * jax.nn.sigmoid(gate) * up).astype(BF16)
    return x + 0.5 * jnp.dot(a_ref[...], wd_ref[...], preferred_element_type=F32)


def _ffn_proj_kernel(x_ref, g_ref, wg_ref, wu_ref, wd_ref, gm_ref, wi_ref,
                     o_ref, q_ref, k_ref, v_ref, p_ref, a_ref):
    y = _swiglu_half_step(x_ref[...], g_ref, wg_ref, wu_ref, wd_ref, a_ref)
    o_ref[...] = y
    h = _rms(y, gm_ref[...]).astype(BF16)
    z = jnp.dot(h, wi_ref[...], preferred_element_type=F32)
    for j in range(N_PAIRS):
        cols = slice(j * LANES, (j + 1) * LANES)
        q_ref[j] = (z[:, 0:D_ATTN][:, cols] * (HEAD_DIM ** -0.5 * LOG2E)).astype(BF16)
        k_ref[j] = z[:, D_ATTN:2 * D_ATTN][:, cols].astype(BF16)
        v_ref[j] = z[:, 2 * D_ATTN:3 * D_ATTN][:, cols].astype(BF16)
    for j in range(D_POOL // LANES):
        p_ref[j] = z[:, 3 * D_ATTN + j * LANES:3 * D_ATTN + (j + 1) * LANES]


def _ffn_proj_call(x, g, wg, wu, wd, gm, wi, layer):
    n = x.shape[0]
    tm = TOKEN_TILE
    tok = lambda d: pl.BlockSpec((tm, d), lambda i: (i, 0))
    blk = lambda d: pl.BlockSpec((d // LANES, tm, LANES), lambda i: (0, i, 0))
    return pl.pallas_call(
        _ffn_proj_kernel,
        out_shape=(
            jax.ShapeDtypeStruct((n, D_MODEL), F32),
            jax.ShapeDtypeStruct((N_PAIRS, n, LANES), BF16),
            jax.ShapeDtypeStruct((N_PAIRS, n, LANES), BF16),
            jax.ShapeDtypeStruct((N_PAIRS, n, LANES), BF16),
            jax.ShapeDtypeStruct((D_POOL // LANES, n, LANES), F32),
        ),
        grid=(n // tm,),
        in_specs=[
            tok(D_MODEL),
            _layer_spec((1, D_MODEL), layer),
            _const_spec((D_MODEL, D_FF)),
            _const_spec((D_MODEL, D_FF)),
            _const_spec((D_FF, D_MODEL)),
            _layer_spec((1, D_MODEL), layer),
            _const_spec((D_MODEL, 3 * D_ATTN + D_POOL)),
        ],
        out_specs=(tok(D_MODEL), blk(D_ATTN), blk(D_ATTN), blk(D_ATTN), blk(D_POOL)),
        scratch_shapes=[pltpu.VMEM((tm, D_FF), BF16)],
        compiler_params=pltpu.CompilerParams(
            dimension_semantics=("parallel",), vmem_limit_bytes=VMEM_LIMIT),
        name="ffn_proj",
    )(x, g, wg, wu, wd, gm, wi)


def _mix_ffn_kernel(x_ref, a_in_ref, m_in_ref, wo_ref, g_ref, wg_ref, wu_ref, wd_ref, fg_ref, *rest,
                    final_norm, n_cast):
    cast_srcs, o_ref, cast_dsts, a_ref = rest[:n_cast], rest[n_cast], rest[n_cast + 1:-1], rest[-1]
    _cast_refs(cast_srcs, cast_dsts)
    mixed = jnp.concatenate([a_in_ref[j] for j in range(a_in_ref.shape[0])]
                            + [m_in_ref[j] for j in range(m_in_ref.shape[0])], axis=1)
    x = x_ref[...] + jnp.dot(mixed, wo_ref[...], preferred_element_type=F32)
    y = _swiglu_half_step(x, g_ref, wg_ref, wu_ref, wd_ref, a_ref)
    if final_norm:
        y = _rms(y, fg_ref[...])
    o_ref[...] = y


def _mix_ffn_call(x, a, m, wo, g, wg, wu, wd, fg, layer, final_norm, cast_srcs, cast_layer):
    n = x.shape[0]
    tm = TOKEN_TILE
    tok = lambda d: pl.BlockSpec((tm, d), lambda i: (i, 0))
    blk = lambda d: pl.BlockSpec((d // LANES, tm, LANES), lambda i: (0, i, 0))
    cast_in, cast_out, cast_shapes = _cast_specs(cast_srcs, cast_layer, n // tm)
    outs = pl.pallas_call(
        functools.partial(_mix_ffn_kernel, final_norm=final_norm, n_cast=len(cast_srcs)),
        out_shape=(jax.ShapeDtypeStruct((n, D_MODEL), F32), *cast_shapes),
        grid=(n // tm,),
        in_specs=[
            tok(D_MODEL), blk(D_ATTN), blk(D_POOL),
            _const_spec((D_ATTN + D_POOL, D_MODEL)),
            _layer_spec((1, D_MODEL), layer),
            _const_spec((D_MODEL, D_FF)),
            _const_spec((D_MODEL, D_FF)),
            _const_spec((D_FF, D_MODEL)),
            _const_spec((1, D_MODEL)),
            *cast_in,
        ],
        out_specs=(tok(D_MODEL), *cast_out),
        scratch_shapes=[pltpu.VMEM((tm, D_FF), BF16)],
        compiler_params=pltpu.CompilerParams(
            dimension_semantics=("arbitrary",), vmem_limit_bytes=VMEM_LIMIT),
        name="mix_ffn",
    )(x, a, m, wo, g, wg, wu, wd, fg, *cast_srcs)
    return outs[0], outs[1:]


def _window_start(r, rows):
    return min(max(r - WIN_ROWS // 2, 0), rows - WIN_ROWS)


def _pair_window_start(r0, rows):
    return min(max(r0 - WIN_ROWS // 2, 0), rows - KEY_ROWS)


def _pair_configs(rows):
    cfgs = {}
    for r0 in range(0, rows, PAIR_ROWS):
        ws = _pair_window_start(r0, rows)
        rel = tuple(_window_start(r0 + rr, rows) - ws for rr in range(PAIR_ROWS))
        assert cfgs.setdefault(r0 - ws, (r0, rel))[1] == rel
    return [(d0, r0) for d0, (r0, _) in sorted(cfgs.items())]


def _pool_delta(p, w, pos=None, seq=None):
    n = p.shape[0]
    half = w // 2

    def shift(x, k):
        rolled = pltpu.roll(x, (-k) % n, 0)
        if pos is None:
            return rolled
        return jnp.where((pos < seq - k) if k > 0 else (pos >= -k), rolled, 0.0)

    fwd = p
    k = 1
    while k < half:
        fwd = fwd + shift(fwd, k)
        k *= 2
    bwd = shift(p, -1)
    k = 1
    while k < half:
        bwd = bwd + shift(bwd, -k)
        k *= 2
    if pos is None:
        return (fwd + bwd) * (1.0 / w) - p
    cnt = jnp.minimum(pos + half, seq) - jnp.maximum(pos - half, 0)
    return (fwd + bwd) / cnt.astype(F32) - p


def _attn_pool_kernel(q_ref, k_ref, v_ref, bias_ref, p_ref, wp_ref, sc_ref, *rest, offsets, n_cast):
    cast_srcs, (a_ref, m_ref), cast_dsts = rest[:n_cast], rest[n_cast:n_cast + 2], rest[n_cast + 2:-4]
    st_ref, e_ref, den_ref, ot_ref = rest[-4:]
    _cast_refs(cast_srcs, cast_dsts)
    n_pairs, seq, _ = q_ref.shape
    rows = seq // GRID_W
    n_keys = KEY_ROWS * GRID_W
    n_q = PAIR_ROWS * GRID_W
    blocks_per_pair = rows // PAIR_ROWS // STAGE_PAIRS
    n_blocks = n_pairs * blocks_per_pair
    lane = lax.broadcasted_iota(jnp.int32, (n_q, LANES), 1)
    first = lane < HEAD_DIM

    def pair_geometry(blk, u):
        blk = jnp.asarray(blk, jnp.int32)
        pair = blk // blocks_per_pair
        r0 = ((blk % blocks_per_pair) * STAGE_PAIRS + u) * PAIR_ROWS
        ws = jnp.clip(r0 - WIN_ROWS // 2, 0, rows - KEY_ROWS)
        d0 = r0 - ws
        cfg = sum((d0 >= t).astype(jnp.int32) for t in offsets[1:])
        return pair, pl.multiple_of(r0 * GRID_W, n_q), pl.multiple_of(ws * GRID_W, GRID_W), cfg

    def scores_stage(blk, slot):
        for u in range(STAGE_PAIRS):
            pair, tok0, key0, cfg = pair_geometry(blk, u)
            qb = q_ref[pair, pl.ds(tok0, n_q), :]
            zero = jnp.zeros_like(qb)
            qm = jnp.concatenate([jnp.where(first, qb, zero), jnp.where(first, zero, qb)], axis=0)
            kw = k_ref[pair, pl.ds(key0, n_keys), :]
            st = lax.dot_general(kw, qm, (((1,), (1,)), ((), ())), preferred_element_type=F32)
            st_ref[slot * STAGE_PAIRS + u] = st + bias_ref[pair, cfg]

    def softmax_stage(slot):
        for u in range(STAGE_PAIRS):
            st = st_ref[slot * STAGE_PAIRS + u]
            mx = jnp.max(st, axis=0, keepdims=True)
            e = jnp.exp2(st - mx)
            den_ref[slot * STAGE_PAIRS + u] = jnp.sum(e, axis=0, keepdims=True)
            e_ref[slot * STAGE_PAIRS + u] = e.astype(BF16)

    def pv_stage(blk, slot):
        for u in range(STAGE_PAIRS):
            pair, _, key0, _ = pair_geometry(blk, u)
            vw = v_ref[pair, pl.ds(key0, n_keys), :]
            ot = lax.dot_general(vw, e_ref[slot * STAGE_PAIRS + u], (((0,), (0,)), ((), ())),
                                 preferred_element_type=F32)
            ot_ref[slot * STAGE_PAIRS + u] = ot / den_ref[slot * STAGE_PAIRS + u]

    def output_stage(blk, slot):
        for u in range(STAGE_PAIRS):
            pair, tok0, _, _ = pair_geometry(blk, u)
            o = ot_ref[slot * STAGE_PAIRS + u].T
            out = jnp.where(first, o[0:n_q], o[n_q:])
            a_ref[pair, pl.ds(tok0, n_q), :] = out.astype(BF16)

    def trip(t, slot):
        static = isinstance(t, int)
        if not static or 3 <= t < n_blocks + 3:
            output_stage(t - 3, 1 - slot)
        if not static or t < n_blocks:
            scores_stage(t, slot)
        if not static or 2 <= t < n_blocks + 2:
            pv_stage(t - 2, slot)
        if not static or 1 <= t < n_blocks + 1:
            softmax_stage(1 - slot)

    n_chunks = seq // POOL_CHUNK
    window = POOL_CHUNK + 2 * POOL_HALO

    def pool_chunk(c):
        if isinstance(c, int):
            start = min(max(c * POOL_CHUNK - POOL_HALO, 0), seq - window)
            pos = start + lax.broadcasted_iota(jnp.int32, (window, LANES), 0)
            keep = c * POOL_CHUNK - start
        else:
            start = pl.multiple_of(c * POOL_CHUNK - POOL_HALO, POOL_HALO)
            pos = None
            keep = POOL_HALO
        for gi, w in enumerate(POOL_WINDOWS):
            d = _pool_delta(p_ref[gi, pl.ds(start, window), :], w, pos, seq)[keep:keep + POOL_CHUNK]
            y = jnp.dot(d.astype(BF16), wp_ref[gi * POOL_GROUP_DIM:(gi + 1) * POOL_GROUP_DIM],
                        preferred_element_type=F32)
            m_ref[gi, pl.ds(c * POOL_CHUNK, POOL_CHUNK), :] = (y * sc_ref[gi]).astype(BF16)

    n_iter = (n_blocks - 4) // 2
    assert n_blocks % 2 == 0 and n_blocks >= 4 and n_iter == n_chunks - 2
    for t in range(3):
        trip(t, t % 2)
    pool_chunk(0)

    def steady(i, carry):
        trip(3 + 2 * i, 1)
        trip(4 + 2 * i, 0)
        pool_chunk(i + 1)
        return carry

    lax.fori_loop(0, n_iter, steady, 0)
    for t in range(n_blocks - 1, n_blocks + 3):
        trip(t, t % 2)
    pool_chunk(n_chunks - 1)


def _attn_pool_call(q, k, v, bias, p, w_pool, pool_scale, layer, batch, seq, cast_srcs):
    n = q.shape[1]
    n_groups = len(POOL_WINDOWS)
    offsets = tuple(d0 for d0, _ in _pair_configs(seq // GRID_W))
    seq_blk = lambda c: pl.BlockSpec((c, seq, LANES), lambda b: (0, b, 0))
    score_shape = (2 * STAGE_PAIRS, KEY_ROWS * GRID_W, 2 * PAIR_ROWS * GRID_W)
    cast_in, cast_out, cast_shapes = _cast_specs(cast_srcs, layer, batch)
    outs = pl.pallas_call(
        functools.partial(_attn_pool_kernel, offsets=offsets, n_cast=len(cast_srcs)),
        out_shape=(
            jax.ShapeDtypeStruct((N_PAIRS, n, LANES), BF16),
            jax.ShapeDtypeStruct((n_groups, n, LANES), BF16),
            *cast_shapes,
        ),
        grid=(batch,),
        in_specs=[
            seq_blk(N_PAIRS), seq_blk(N_PAIRS), seq_blk(N_PAIRS),
            _layer_spec(bias.shape[1:], layer),
            seq_blk(n_groups),
            _const_spec((n_groups * POOL_GROUP_DIM, POOL_GROUP_DIM)),
            _layer_spec((n_groups, 1, POOL_GROUP_DIM), layer),
            *cast_in,
        ],
        out_specs=(seq_blk(N_PAIRS), seq_blk(n_groups), *cast_out),
        scratch_shapes=[
            pltpu.VMEM(score_shape, F32),
            pltpu.VMEM(score_shape, BF16),
            pltpu.VMEM((2 * STAGE_PAIRS, 1, score_shape[2]), F32),
            pltpu.VMEM((2 * STAGE_PAIRS, LANES, score_shape[2]), F32),
        ],
        compiler_params=pltpu.CompilerParams(
            dimension_semantics=("parallel",), vmem_limit_bytes=VMEM_LIMIT),
        name="attn_pool",
    )(q, k, v, bias, p, w_pool, pool_scale, *cast_srcs)
    return outs[0], outs[1], outs[2:]


def _bias_kernel(rpb_ref, o_ref, tab_ref, *, rows, configs):
    layer = pl.program_id(0)
    pair = pl.program_id(1)
    n_dr = 2 * WIN_ROWS - 1
    n_dc = 2 * WIN_COLS - 1
    sublanes = 8
    sub = lax.broadcasted_iota(jnp.int32, (sublanes, LANES), 0)
    lane8 = lax.broadcasted_iota(jnp.int32, (sublanes, LANES), 1)
    skew = (n_dc - 1) - lane8 + sub
    for hh in range(2):
        base = ((layer * N_HEADS + 2 * pair + hh) * n_dr) * n_dc
        for dr in range(n_dr):
            vec = jnp.zeros((sublanes, LANES), F32)
            for dc in range(n_dc):
                vec = jnp.where(skew == dc, rpb_ref[base + dr * n_dc + dc], vec)
            for g in range(GRID_W // sublanes):
                for rr in range(PAIR_ROWS):
                    shift = (sublanes * g - (WIN_COLS - 1) + GRID_W * rr) % LANES
                    rows8 = pl.ds(dr * GRID_W + sublanes * g, sublanes)
                    tab_ref[hh, rr, rows8, :] = pltpu.roll(vec, shift, 1)

    kc = lax.broadcasted_iota(jnp.int32, (GRID_W, LANES), 0)
    lane = lax.broadcasted_iota(jnp.int32, (GRID_W, LANES), 1)
    qc = lane % GRID_W
    cs = jnp.clip(qc - WIN_COLS // 2, 0, GRID_W - WIN_COLS)
    in_cols = (kc >= cs) & (kc < cs + WIN_COLS)
    for ci, (d0, r0) in enumerate(configs):
        ws = r0 - d0
        for i in range(KEY_ROWS):
            key_row = ws + i
            for hh in range(2):
                halves = []
                for rr in range(PAIR_ROWS):
                    rs = _window_start(r0 + rr, rows)
                    if rs <= key_row < rs + WIN_ROWS:
                        dr = key_row - (r0 + rr) + WIN_ROWS - 1
                        halves.append(tab_ref[hh, rr, dr * GRID_W:(dr + 1) * GRID_W, :])
                    else:
                        halves.append(jnp.full((GRID_W, LANES), NEG_INF, F32))
                tile = jnp.where(in_cols, jnp.where(lane < GRID_W, halves[0], halves[1]), NEG_INF)
                o_ref[ci, i * GRID_W:(i + 1) * GRID_W, hh * LANES:(hh + 1) * LANES] = tile * LOG2E


def _bias_call(rpb_flat, depth, rows):
    configs = tuple(_pair_configs(rows))
    slab = (len(configs), KEY_ROWS * GRID_W, 2 * PAIR_ROWS * GRID_W)
    return pl.pallas_call(
        functools.partial(_bias_kernel, rows=rows, configs=configs),
        out_shape=jax.ShapeDtypeStruct((depth, N_PAIRS) + slab, F32),
        grid=(depth, N_PAIRS),
        in_specs=[pl.BlockSpec(memory_space=pltpu.SMEM)],
        out_specs=pl.BlockSpec((None, None) + slab, lambda l, j: (l, j, 0, 0, 0)),
        scratch_shapes=[pltpu.VMEM((2, PAIR_ROWS, (2 * WIN_ROWS - 1) * GRID_W, LANES), F32)],
        compiler_params=pltpu.CompilerParams(
            dimension_semantics=("parallel", "parallel"), vmem_limit_bytes=VMEM_LIMIT),
        name="bias_table",
    )(rpb_flat)


def kernel(x, ffn1_norm, ffn1_w_gate, ffn1_w_up, ffn1_w_down, mix_norm, w_in, rpb, w_pool, pool_scale, w_out, ffn2_norm, ffn2_w_gate, ffn2_w_up, ffn2_w_down, final_norm):
    batch, seq, d = x.shape
    depth = w_in.shape[0]
    n_groups = len(POOL_WINDOWS)
    x = x.reshape(batch * seq, d)
    first = (ffn1_w_gate, ffn1_w_up, ffn1_w_down, w_in,
             w_pool.reshape(depth, n_groups * POOL_GROUP_DIM, POOL_GROUP_DIM))
    second = (w_out, ffn2_w_gate, ffn2_w_up, ffn2_w_down)
    g1 = ffn1_norm.reshape(depth, 1, d)
    gm = mix_norm.reshape(depth, 1, d)
    g2 = ffn2_norm.reshape(depth, 1, d)
    fg = final_norm.reshape(1, d)
    scale = pool_scale.reshape(depth, n_groups, 1, POOL_GROUP_DIM)
    bias = _bias_call(rpb.reshape(-1), depth, seq // GRID_W)
    wg1, wu1, wd1, wi, wp = _cast_call(first, 0)
    for l in range(depth):
        x, q, k, v, p = _ffn_proj_call(x, g1, wg1, wu1, wd1, gm, wi, l)
        a, m, (wo, wg2, wu2, wd2) = _attn_pool_call(q, k, v, bias, p, wp, scale, l, batch, seq, second)
        ahead = first if l + 1 < depth else ()
        x, cast = _mix_ffn_call(x, a, m, wo, g2, wg2, wu2, wd2, fg, l, l == depth - 1, ahead, l + 1)
        if ahead:
            wg1, wu1, wd1, wi, wp = cast
    return x.reshape(batch, seq, d)
```
